```python
import math
import jax, jax.numpy as jnp
from jax import lax
import numpy as np


D_MODEL = 1024
BATCH = 4
SEQ = 8192
DEPTH = 4

N_MIXERS = 3
MEM_LEN = 256
EPS = 1e-6

A_HEADS = 4
A_DQK = D_MODEL // 8
A_DV = D_MODEL // A_HEADS
A_CHUNK = 64
A_CONV = 4
A_IN = 2 * A_HEADS * A_DQK + 2 * A_HEADS * A_DV + 2 * A_HEADS

B_HEADS = 4
B_DK = D_MODEL // 8
B_DV = D_MODEL // B_HEADS
B_RANK = 16
B_TAU = 16.0
B_CHUNK = 32
B_IN = 2 * B_HEADS * B_DK + 2 * B_HEADS * B_DV + B_RANK

C_GROUP = 16
C_GROUPS = D_MODEL // C_GROUP
C_STATE = 64
DT_MIN = 1e-3
DT_MAX = 1e-1

X_HEADS = 4
X_DH = D_MODEL // X_HEADS

D_FF = ((8 * D_MODEL + 3 * 256 - 1) // (3 * 256)) * 256

kernel_name = 'hybrid_mlstm_gla_s5_xattn_swiglu'

F32 = jnp.float32


def rmsnorm(x, g):
    xf = x.astype(F32)
    xf = xf * lax.rsqrt(jnp.mean(xf * xf, axis=-1, keepdims=True) + EPS)
    return (xf * g.astype(F32)).astype(x.dtype)


def head_rmsnorm(h, g):
    h = h * lax.rsqrt(jnp.mean(h * h, axis=-1, keepdims=True) + EPS)
    return h.reshape(h.shape[0], h.shape[1], -1) * g.astype(F32)


def causal_dwconv(x, w):
    K = w.shape[0]
    S = x.shape[1]
    xp = jnp.pad(x, ((0, 0), (K - 1, 0), (0, 0)))
    y = xp[:, 0:S] * w[0]
    for j in range(1, K):
        y = y + xp[:, j:j + S] * w[j]
    return y


def to_chunks(t, L):
    Bsz, S, H, d = t.shape
    return t.reshape(Bsz, S // L, L, H, d).transpose(1, 0, 3, 2, 4)


def from_chunks(t):
    nc, Bsz, H, L, d = t.shape
    return t.transpose(1, 0, 3, 2, 4).reshape(Bsz, nc * L, H, d)


def mlstm_mix(xn, w_in, conv_w, gate_b, norm_g, w_out):
    Bsz, S, _ = xn.shape
    H, dk, dv, L = A_HEADS, A_DQK, A_DV, A_CHUNK
    nc = S // L
    proj = xn @ w_in
    qk, v, o, g = jnp.split(proj, [2 * H * dk, 2 * H * dk + H * dv, 2 * H * dk + 2 * H * dv], axis=-1)
    qk = jax.nn.silu(causal_dwconv(qk, conv_w)).astype(F32)
    q = qk[..., :H * dk].reshape(Bsz, S, H, dk) * (dk ** -0.5)
    k = qk[..., H * dk:].reshape(Bsz, S, H, dk)
    v = v.astype(F32).reshape(Bsz, S, H, dv)
    g = g.astype(F32) + gate_b.astype(F32)
    i_pre = g[..., :H]
    log_f = jax.nn.log_sigmoid(g[..., H:])
    gate_chunks = lambda t: t.reshape(Bsz, nc, L, H).transpose(1, 0, 3, 2)
    causal = jnp.tril(jnp.ones((L, L), dtype=bool))

    def step(carry, inp):
        C, n, m = carry
        qc, kc, vc, ic, fc = inp
        b = jnp.cumsum(fc, axis=-1)
        dmat = b[..., :, None] - b[..., None, :] + ic[..., None, :]
        dmat = jnp.where(causal, dmat, -jnp.inf)
        inter = b + m[..., None]
        m_t = jnp.maximum(inter, jnp.max(dmat, axis=-1))
        w_intra = jnp.exp(dmat - m_t[..., None])
        w_inter = jnp.exp(inter - m_t)
        s = jnp.einsum('bhtd,bhsd->bhts', qc, kc) * w_intra
        num = jnp.einsum('bhts,bhsv->bhtv', s, vc) + w_inter[..., None] * jnp.einsum('bhtd,bhdv->bhtv', qc, C)
        den = jnp.sum(s, axis=-1) + w_inter * jnp.einsum('bhtd,bhd->bht', qc, n)
        h = num / jnp.maximum(jnp.abs(den), jnp.exp(-m_t))[..., None]
        g_tot = b[..., -1]
        a = g_tot[..., None] - b + ic
        m_new = jnp.maximum(g_tot + m, jnp.max(a, axis=-1))
        decay = jnp.exp(g_tot + m - m_new)
        wa = jnp.exp(a - m_new[..., None])
        C_new = decay[..., None, None] * C + jnp.einsum('bhs,bhsd,bhsv->bhdv', wa, kc, vc)
        n_new = decay[..., None] * n + jnp.einsum('bhs,bhsd->bhd', wa, kc)
        return (C_new, n_new, m_new), h

    init = (jnp.zeros((Bsz, H, dk, dv), F32), jnp.zeros((Bsz, H, dk), F32), jnp.zeros((Bsz, H), F32))
    _, h = lax.scan(step, init, (to_chunks(q, L), to_chunks(k, L), to_chunks(v, L),
                                 gate_chunks(i_pre), gate_chunks(log_f)))
    h = head_rmsnorm(from_chunks(h), norm_g) * jax.nn.sigmoid(o.astype(F32))
    return h.astype(xn.dtype) @ w_out


def gla_mix(xn, w_in, gate_w2, gate_b, norm_g, w_out):
    Bsz, S, _ = xn.shape
    H, dk, dv, L = B_HEADS, B_DK, B_DV, B_CHUNK
    proj = xn @ w_in
    q, k, v, r, g_low = jnp.split(proj, [H * dk, 2 * H * dk, 2 * H * dk + H * dv, 2 * H * dk + 2 * H * dv], axis=-1)
    log_a = jax.nn.log_sigmoid((g_low @ gate_w2 + gate_b).astype(F32)) / B_TAU
    q = q.astype(F32).reshape(Bsz, S, H, dk) * (dk ** -0.5)
    k = k.astype(F32).reshape(Bsz, S, H, dk)
    v = v.astype(F32).reshape(Bsz, S, H, dv)
    log_a = log_a.reshape(Bsz, S, H, dk)
    causal = jnp.tril(jnp.ones((L, L), dtype=bool))

    def step(state, inp):
        qc, kc, vc, lac = inp
        b = jnp.cumsum(lac, axis=2)
        rel = b[:, :, :, None, :] - b[:, :, None, :, :]
        rel = jnp.where(causal[:, :, None], rel, -jnp.inf)
        s = jnp.einsum('bhtd,bhsd,bhtsd->bhts', qc, kc, jnp.exp(rel))
        o = jnp.einsum('bhts,bhsv->bhtv', s, vc) + jnp.einsum('bhtd,bhdv->bhtv', qc * jnp.exp(b), state)
        b_last = b[:, :, -1:, :]
        k_dec = kc * jnp.exp(b_last - b)
        state = jnp.exp(b_last[:, :, 0, :])[..., None] * state + jnp.einsum('bhsd,bhsv->bhdv', k_dec, vc)
        return state, o

    _, o = lax.scan(step, jnp.zeros((Bsz, H, dk, dv), F32),
                    (to_chunks(q, L), to_chunks(k, L), to_chunks(v, L), to_chunks(log_a, L)))
    o = head_rmsnorm(from_chunks(o), norm_g) * jax.nn.silu(r.astype(F32))
    return o.astype(xn.dtype) @ w_out


def complex_affine_combine(e1, e2):
    ar1, ai1, br1, bi1 = e1
    ar2, ai2, br2, bi2 = e2
    ar = ar1 * ar2 - ai1 * ai2
    ai = ar1 * ai2 + ai1 * ar2
    br = ar2 * br1 - ai2 * bi1 + br2
    bi = ar2 * bi1 + ai2 * br1 + bi2
    return ar, ai, br, bi


def s5_mix(xn, w_in, lam_re, lam_im, log_dt, b_re, b_im, c_re, c_im, d_skip, w_gate, b_gate, w_out):
    Bsz, S, _ = xn.shape
    G, P, Hg = C_GROUPS, C_STATE, C_GROUP
    u = (xn @ w_in).astype(F32).reshape(Bsz, S, G, Hg)
    lr = lam_re.astype(F32)
    li = lam_im.astype(F32)
    dt = jnp.exp(log_dt.astype(F32))[:, None]
    mag = jnp.exp(lr * dt)
    ab_re = mag * jnp.cos(li * dt)
    ab_im = mag * jnp.sin(li * dt)
    den = lr * lr + li * li
    z_re = ((ab_re - 1.0) * lr + ab_im * li) / den
    z_im = (ab_im * lr - (ab_re - 1.0) * li) / den
    br = b_re.astype(F32)
    bi = b_im.astype(F32)
    bb_re = z_re[..., None] * br - z_im[..., None] * bi
    bb_im = z_re[..., None] * bi + z_im[..., None] * br
    bu_re = jnp.einsum('bsgh,gph->bsgp', u, bb_re)
    bu_im = jnp.einsum('bsgh,gph->bsgp', u, bb_im)
    a_re = jnp.broadcast_to(ab_re, (1, S, G, P))
    a_im = jnp.broadcast_to(ab_im, (1, S, G, P))
    _, _, s_re, s_im = lax.associative_scan(complex_affine_combine, (a_re, a_im, bu_re, bu_im), axis=1)
    y = (jnp.einsum('bsgp,ghp->bsgh', s_re, c_re.astype(F32))
         - jnp.einsum('bsgp,ghp->bsgh', s_im, c_im.astype(F32)))
    y = y + d_skip.astype(F32).reshape(G, Hg) * u
    y = jax.nn.gelu(y.reshape(Bsz, S, G * Hg)).astype(xn.dtype)
    y = y * jax.nn.sigmoid(y @ w_gate + b_gate)
    return y @ w_out


def cross_attn(xn, memn, w_q, w_kv, w_o):
    Bsz, S, D = xn.shape
    M = memn.shape[1]
    q = (xn @ w_q).reshape(Bsz, S, X_HEADS, X_DH)
    k, v = jnp.split(memn @ w_kv, 2, axis=-1)
    k = k.reshape(Bsz, M, X_HEADS, X_DH)
    v = v.reshape(Bsz, M, X_HEADS, X_DH)
    s = jnp.einsum('bshd,bmhd->bhsm', q, k).astype(F32) * (X_DH ** -0.5)
    p = jax.nn.softmax(s, axis=-1).astype(v.dtype)
    o = jnp.einsum('bhsm,bmhd->bshd', p, v).reshape(Bsz, S, D)
    return o @ w_o


def swiglu(xn, w_gate, w_up, w_down):
    return (jax.nn.silu(xn @ w_gate) * (xn @ w_up)) @ w_down


def setup_inputs(seed: int = 0) -> dict:
    key = jax.random.key(seed)
    ks = iter(jax.random.split(key, 64))

    def nrm(shape, scale):
        return jax.random.normal(next(ks), shape, F32) * scale

    def gain(shape):
        return 1.0 + nrm(shape, 0.02)

    D = D_MODEL
    n_a = len(range(0, DEPTH, N_MIXERS))
    n_b = len(range(1, DEPTH, N_MIXERS))
    n_c = len(range(2, DEPTH, N_MIXERS))
    d = {}
    d['x'] = nrm((BATCH, SEQ, D), 1.0)
    d['mem'] = nrm((BATCH, MEM_LEN, D), 1.0)
    d['norm_mix'] = gain((DEPTH, D))
    d['norm_x'] = gain((DEPTH, D))
    d['norm_ffn'] = gain((DEPTH, D))
    d['norm_mem'] = gain((D,))
    d['norm_final'] = gain((D,))
    d['a_w_in'] = nrm((n_a, D, A_IN), D ** -0.5)
    d['a_conv_w'] = nrm((n_a, A_CONV, 2 * A_HEADS * A_DQK), A_CONV ** -0.5)
    i_bias = nrm((n_a, A_HEADS), 0.1)
    f_bias = jnp.linspace(3.0, 6.0, A_HEADS, dtype=F32)[None, :] + nrm((n_a, A_HEADS), 0.1)
    d['a_gate_b'] = jnp.concatenate([i_bias, f_bias], axis=-1)
    d['a_norm'] = gain((n_a, A_HEADS * A_DV))
    d['a_w_out'] = nrm((n_a, A_HEADS * A_DV, D), (A_HEADS * A_DV) ** -0.5)
    d['b_w_in'] = nrm((n_b, D, B_IN), D ** -0.5)
    d['b_gate_w2'] = nrm((n_b, B_RANK, B_HEADS * B_DK), B_RANK ** -0.5)
    d['b_gate_b'] = nrm((n_b, B_HEADS * B_DK), 0.1)
    d['b_norm'] = gain((n_b, B_HEADS * B_DV))
    d['b_w_out'] = nrm((n_b, B_HEADS * B_DV, D), (B_HEADS * B_DV) ** -0.5)
    d['c_w_in'] = nrm((n_c, D, D), D ** -0.5)
    d['c_lam_re'] = -0.5 + nrm((n_c, C_GROUPS, C_STATE), 0.01)
    d['c_lam_im'] = math.pi * jnp.arange(C_STATE, dtype=F32)[None, None, :] + nrm((n_c, C_GROUPS, C_STATE), 0.01)
    d['c_log_dt'] = jax.random.uniform(next(ks), (n_c, C_GROUPS), F32, math.log(DT_MIN), math.log(DT_MAX))
    d['c_b_re'] = nrm((n_c, C_GROUPS, C_STATE, C_GROUP), (2 * C_GROUP) ** -0.5)
    d['c_b_im'] = nrm((n_c, C_GROUPS, C_STATE, C_GROUP), (2 * C_GROUP) ** -0.5)
    d['c_c_re'] = nrm((n_c, C_GROUPS, C_GROUP, C_STATE), 0.25)
    d['c_c_im'] = nrm((n_c, C_GROUPS, C_GROUP, C_STATE), 0.25)
    d['c_d'] = nrm((n_c, D), 0.5)
    d['c_w_gate'] = nrm((n_c, D, D), D ** -0.5)
    d['c_b_gate'] = nrm((n_c, D), 0.02)
    d['c_w_out'] = nrm((n_c, D, D), D ** -0.5)
    d['x_w_q'] = nrm((DEPTH, D, D), D ** -0.5)
    d['x_w_kv'] = nrm((DEPTH, D, 2 * D), D ** -0.5)
    d['x_w_o'] = nrm((DEPTH, D, D), D ** -0.5)
    d['f_w_gate'] = nrm((DEPTH, D, D_FF), D ** -0.5)
    d['f_w_up'] = nrm((DEPTH, D, D_FF), D ** -0.5)
    d['f_w_down'] = nrm((DEPTH, D_FF, D), D_FF ** -0.5)
    return d


def reference(x, mem, norm_mix, norm_x, norm_ffn, norm_mem, norm_final,
              a_w_in, a_conv_w, a_gate_b, a_norm, a_w_out,
              b_w_in, b_gate_w2, b_gate_b, b_norm, b_w_out,
              c_w_in, c_lam_re, c_lam_im, c_log_dt, c_b_re, c_b_im, c_c_re, c_c_im,
              c_d, c_w_gate, c_b_gate, c_w_out,
              x_w_q, x_w_kv, x_w_o,
              f_w_gate, f_w_up, f_w_down):
    memn = rmsnorm(mem, norm_mem)
    h = x
    for i in range(DEPTH):
        kind = i % N_MIXERS
        j = i // N_MIXERS
        hn = rmsnorm(h, norm_mix[i])
        if kind == 0:
            h = h + mlstm_mix(hn, a_w_in[j], a_conv_w[j], a_gate_b[j], a_norm[j], a_w_out[j])
        elif kind == 1:
            h = h + gla_mix(hn, b_w_in[j], b_gate_w2[j], b_gate_b[j], b_norm[j], b_w_out[j])
        else:
            h = h + s5_mix(hn, c_w_in[j], c_lam_re[j], c_lam_im[j], c_log_dt[j], c_b_re[j], c_b_im[j],
                           c_c_re[j], c_c_im[j], c_d[j], c_w_gate[j], c_b_gate[j], c_w_out[j])
        h = h + cross_attn(rmsnorm(h, norm_x[i]), memn, x_w_q[i], x_w_kv[i], x_w_o[i])
        h = h + swiglu(rmsnorm(h, norm_ffn[i]), f_w_gate[i], f_w_up[i], f_w_down[i])
    return rmsnorm(h, norm_final)
```

```python
import functools
import math

import jax
import jax.numpy as jnp
from jax import lax
from jax.experimental import pallas as pl
from jax.experimental.pallas import tpu as pltpu

F32 = jnp.float32
BF16 = jnp.bfloat16
EPS = 1e-6

N_MIXERS = 3
A_HEADS = 4
A_CONV = 4
B_HEADS = 4
B_RANK = 16
B_TAU = 16.0
C_GROUP = 16
C_STATE = 64
X_HEADS = 4

LANES = 128
SUBLANES = 8
VMEM_LIMIT_BYTES = 56 * 1024 * 1024

ROW_TILE = 512
SCAN_TILE = 256
MIX_CHUNK = 128
FF_CHUNK = 256
S5_BLOCK_GROUPS = LANES // C_GROUP


def _params(semantics):
    return pltpu.CompilerParams(dimension_semantics=semantics, vmem_limit_bytes=VMEM_LIMIT_BYTES)


def _whole(shape):
    zeros = (0,) * len(shape)
    return pl.BlockSpec(shape, lambda *_: zeros)


def _rms(x, g):
    return x * lax.rsqrt(jnp.mean(x * x, axis=-1, keepdims=True) + EPS) * g


def _dot(a, b):
    return jnp.dot(a, b, preferred_element_type=F32)


def _dot_nt(a, b):
    return lax.dot_general(a, b, (((1,), (1,)), ((), ())), preferred_element_type=F32)


def _dot_tn(a, b):
    return lax.dot_general(a, b, (((0,), (0,)), ((), ())), preferred_element_type=F32)


def _split3(x):
    hi = x.astype(BF16)
    r = x - hi.astype(F32)
    mid = r.astype(BF16)
    lo = (r - mid.astype(F32)).astype(BF16)
    return hi, mid, lo


def _dot_exact_lhs01(m01, x):
    hi, mid, lo = _split3(x)
    return _dot(m01, hi) + _dot(m01, mid) + _dot(m01, lo)


def _head_rms(o, g):
    return o * lax.rsqrt(jnp.mean(o * o, axis=-1, keepdims=True) + EPS) * g


def _rms_matmul_kernel(x_ref, g_ref, w_ref, o_ref):
    xn = _rms(x_ref[...], g_ref[...]).astype(BF16)
    o_ref[...] = _dot(xn, w_ref[...]).astype(o_ref.dtype)


def _rms_matmul(x, g, w, tm, out_dtype):
    rows, d = x.shape
    n = w.shape[1]
    return pl.pallas_call(
        _rms_matmul_kernel,
        grid=(rows // tm,),
        in_specs=[pl.BlockSpec((tm, d), lambda i: (i, 0)), _whole((1, d)), _whole((d, n))],
        out_specs=pl.BlockSpec((tm, n), lambda i: (i, 0)),
        out_shape=jax.ShapeDtypeStruct((rows, n), out_dtype),
        compiler_params=_params(("parallel",)),
        name="mem_kv_proj",
    )(x, g, w)


def _xattn_kernel(h_ref, g_ref, wq_ref, kv_ref, wo_ref, o_ref):
    d = h_ref.shape[1]
    dh = d // X_HEADS
    x = h_ref[...]
    xn = _rms(x, g_ref[...]).astype(BF16)
    q = _dot(xn, wq_ref[...]).astype(BF16)
    outs = []
    for hd in range(X_HEADS):
        qh = q[:, hd * dh:(hd + 1) * dh]
        kh = kv_ref[:, hd * dh:(hd + 1) * dh]
        vh = kv_ref[:, d + hd * dh:d + (hd + 1) * dh]
        s = _dot_nt(qh, kh) * (dh ** -0.5)
        e = jnp.exp(s - jnp.max(s, axis=-1, keepdims=True))
        p = e / jnp.sum(e, axis=-1, keepdims=True)
        outs.append(_dot(p.astype(BF16), vh))
    o = jnp.concatenate(outs, axis=-1).astype(BF16)
    o_ref[...] = x + _dot(o, wo_ref[...])


def _xattn(h, g, wq, kv, layer, wo, batch, mem_len):
    rows, d = h.shape
    tm = ROW_TILE
    per_batch = rows // batch // tm
    return pl.pallas_call(
        _xattn_kernel,
        grid=(batch, per_batch),
        in_specs=[
            pl.BlockSpec((tm, d), lambda b, i: (b * per_batch + i, 0)),
            _whole((1, d)),
            _whole((d, d)),
            pl.BlockSpec((mem_len, 2 * d), lambda b, i: (b, layer)),
            _whole((d, d)),
        ],
        out_specs=pl.BlockSpec((tm, d), lambda b, i: (b * per_batch + i, 0)),
        out_shape=jax.ShapeDtypeStruct((rows, d), F32),
        compiler_params=_params(("parallel", "parallel")),
        name="xattn",
    )(h, g, wq, kv, wo)


def _swiglu_kernel(h_ref, g_ref, wg_ref, wu_ref, wd_ref, gf_ref, o_ref, *, final_norm):
    x = h_ref[...]
    xn = _rms(x, g_ref[...]).astype(BF16)
    o_ref[...] = x
    for c in range(wg_ref.shape[0]):
        gate = _dot(xn, wg_ref[c])
        up = _dot(xn, wu_ref[c])
        act = (jax.nn.silu(gate) * up).astype(BF16)
        o_ref[...] += _dot(act, wd_ref[c])
    if final_norm:
        o_ref[...] = _rms(o_ref[...], gf_ref[...])


def _swiglu(h, g, wg, wu, wd, gf, final_norm):
    rows, d = h.shape
    tm = ROW_TILE
    return pl.pallas_call(
        functools.partial(_swiglu_kernel, final_norm=final_norm),
        grid=(rows // tm,),
        in_specs=[pl.BlockSpec((tm, d), lambda i: (i, 0)), _whole((1, d)), _whole(wg.shape),
                  _whole(wu.shape), _whole(wd.shape), _whole((1, d))],
        out_specs=pl.BlockSpec((tm, d), lambda i: (i, 0)),
        out_shape=jax.ShapeDtypeStruct((rows, d), F32),
        compiler_params=_params(("parallel",)),
        name="swiglu",
    )(h, g, wg, wu, wd, gf)


def _mlstm_in_kernel(h_ref, g_ref, wqk_ref, wv_ref, wo_ref, wgt_ref, cw_ref, gb_ref,
                     q_ref, k_ref, v_ref, og_ref, gr_ref, xp_ref, *, tiles_per_seq):
    tm, d = h_ref.shape
    nqk = wqk_ref.shape[1]
    dk = nqk // (2 * A_HEADS)
    xn = _rms(h_ref[...], g_ref[...]).astype(BF16)
    qk = _dot(xn, wqk_ref[...])

    @pl.when(pl.program_id(0) % tiles_per_seq == 0)
    def _():
        xp_ref[0:SUBLANES, :] = jnp.zeros((SUBLANES, nqk), F32)

    xp_ref[SUBLANES:SUBLANES + tm, :] = qk
    cw = cw_ref[...]
    y = cw[A_CONV - 1:A_CONV, :] * qk
    for j in range(A_CONV - 1):
        y = y + cw[j:j + 1, :] * xp_ref[pl.ds(SUBLANES - (A_CONV - 1) + j, tm), :]
    xp_ref[0:SUBLANES, :] = qk[tm - SUBLANES:tm, :]
    y = jax.nn.silu(y)
    q_ref[...] = (y[:, :nqk // 2] * (dk ** -0.5)).astype(BF16)
    k_ref[...] = y[:, nqk // 2:].astype(BF16)
    v_ref[...] = _dot(xn, wv_ref[...]).astype(BF16)
    og_ref[...] = jax.nn.sigmoid(_dot(xn, wo_ref[...])).astype(BF16)
    gr = _dot_nt(wgt_ref[...], xn) + gb_ref[...]
    is_input_gate = lax.broadcasted_iota(jnp.int32, gr.shape, 0) < A_HEADS
    gr_ref[...] = jnp.where(is_input_gate, gr, jax.nn.log_sigmoid(gr))


def _mlstm_scan_kernel(q_ref, k_ref, v_ref, og_ref, gr_ref, ng_ref, wout_ref, h_ref, o_ref,
                       c_ref, n_ref, m_ref, y_ref):
    ts = q_ref.shape[0]
    dk = q_ref.shape[1] // A_HEADS
    dv = v_ref.shape[1] // A_HEADS
    L = MIX_CHUNK

    @pl.when(pl.program_id(1) == 0)
    def _():
        c_ref[...] = jnp.zeros(c_ref.shape, F32)
        n_ref[...] = jnp.zeros(n_ref.shape, F32)
        m_ref[...] = jnp.zeros(m_ref.shape, F32)

    row = lax.broadcasted_iota(jnp.int32, (L, L), 0)
    col = lax.broadcasted_iota(jnp.int32, (L, L), 1)
    causal = col <= row
    eye = col == row

    def to_col(x_row):
        return jnp.sum(jnp.where(eye, x_row, 0.0), axis=1, keepdims=True)

    for c in range(ts // L):
        rows = slice(c * L, (c + 1) * L)
        for hd in range(A_HEADS):
            i_row = gr_ref[hd:hd + 1, rows]
            f_row = gr_ref[A_HEADS + hd:A_HEADS + hd + 1, rows]
            b_col = jnp.sum(jnp.where(causal, f_row, 0.0), axis=1, keepdims=True)
            b_row = jnp.sum(jnp.where(eye, b_col, 0.0), axis=0, keepdims=True)
            i_col = to_col(i_row)
            m_prev = m_ref[hd][0:1, 0:1]
            dmat = jnp.where(causal, b_col - b_row + i_row, -jnp.inf)
            inter = b_col + m_prev
            m_t = jnp.maximum(inter, jnp.max(dmat, axis=1, keepdims=True))
            w_intra = jnp.exp(dmat - m_t)
            w_inter = jnp.exp(inter - m_t)
            qh = q_ref[rows, hd * dk:(hd + 1) * dk]
            kh = k_ref[rows, hd * dk:(hd + 1) * dk]
            vh = v_ref[rows, hd * dv:(hd + 1) * dv]
            c_prev = c_ref[hd]
            n_prev = n_ref[hd][0:1, :]
            s = _dot_nt(qh, kh) * w_intra
            num = _dot(s.astype(BF16), vh) + w_inter * _dot(qh, c_prev.astype(BF16))
            qn = jnp.sum(qh.astype(F32) * n_prev, axis=1, keepdims=True)
            den = jnp.sum(s, axis=1, keepdims=True) + w_inter * qn
            hh = num / jnp.maximum(jnp.abs(den), jnp.exp(-m_t))
            g_tot = jnp.sum(f_row, axis=1, keepdims=True)
            a_col = g_tot - b_col + i_col
            m_new = jnp.maximum(g_tot + m_prev, jnp.max(a_col, axis=0, keepdims=True))
            decay = jnp.exp(g_tot + m_prev - m_new)
            kw = kh.astype(F32) * jnp.exp(a_col - m_new)
            c_ref[hd] = decay * c_prev + _dot_tn(kw.astype(BF16), vh)
            n_new = decay * n_prev + jnp.sum(kw, axis=0, keepdims=True)
            n_ref[hd] = jnp.broadcast_to(n_new, n_ref.shape[1:])
            m_ref[hd] = jnp.broadcast_to(m_new, m_ref.shape[1:])
            cols = slice(hd * dv, (hd + 1) * dv)
            yh = _head_rms(hh, ng_ref[:, cols]) * og_ref[rows, cols].astype(F32)
            y_ref[rows, cols] = yh.astype(BF16)

    o_ref[...] = h_ref[...] + _dot(y_ref[...], wout_ref[...])


def _mlstm_layer(h, gain, w_in, conv_w, gate_b, norm_g, w_out, batch):
    rows, d = h.shape
    seq = rows // batch
    nqk = conv_w.shape[1]
    dk = nqk // (2 * A_HEADS)
    dv = d // A_HEADS
    nv = A_HEADS * dv
    ng = 2 * A_HEADS
    wqk = w_in[:, :nqk].astype(BF16)
    wv = w_in[:, nqk:nqk + nv].astype(BF16)
    wo = w_in[:, nqk + nv:nqk + 2 * nv].astype(BF16)
    wgt = w_in[:, nqk + 2 * nv:].T.astype(BF16)
    tm = ROW_TILE
    q, k, v, og, gr = pl.pallas_call(
        functools.partial(_mlstm_in_kernel, tiles_per_seq=seq // tm),
        grid=(rows // tm,),
        in_specs=[pl.BlockSpec((tm, d), lambda i: (i, 0)), _whole((1, d)), _whole((d, nqk)),
                  _whole((d, nv)), _whole((d, nv)), _whole((ng, d)), _whole((A_CONV, nqk)),
                  _whole((ng, 1))],
        out_specs=[pl.BlockSpec((tm, nqk // 2), lambda i: (i, 0)),
                   pl.BlockSpec((tm, nqk // 2), lambda i: (i, 0)),
                   pl.BlockSpec((tm, nv), lambda i: (i, 0)),
                   pl.BlockSpec((tm, nv), lambda i: (i, 0)),
                   pl.BlockSpec((ng, tm), lambda i: (0, i))],
        out_shape=[jax.ShapeDtypeStruct((rows, nqk // 2), BF16),
                   jax.ShapeDtypeStruct((rows, nqk // 2), BF16),
                   jax.ShapeDtypeStruct((rows, nv), BF16),
                   jax.ShapeDtypeStruct((rows, nv), BF16),
                   jax.ShapeDtypeStruct((ng, rows), F32)],
        scratch_shapes=[pltpu.VMEM((tm + SUBLANES, nqk), F32)],
        compiler_params=_params(("arbitrary",)),
        name="mlstm_in",
    )(h, gain, wqk, wv, wo, wgt, conv_w, gate_b.reshape(ng, 1))

    ts = SCAN_TILE
    per_batch = seq // ts
    tile = lambda b, i: (b * per_batch + i, 0)
    return pl.pallas_call(
        _mlstm_scan_kernel,
        grid=(batch, per_batch),
        in_specs=[pl.BlockSpec((ts, nqk // 2), tile), pl.BlockSpec((ts, nqk // 2), tile),
                  pl.BlockSpec((ts, nv), tile), pl.BlockSpec((ts, nv), tile),
                  pl.BlockSpec((ng, ts), lambda b, i: (0, b * per_batch + i)),
                  _whole((1, nv)), _whole((nv, d)), pl.BlockSpec((ts, d), tile)],
        out_specs=pl.BlockSpec((ts, d), tile),
        out_shape=jax.ShapeDtypeStruct((rows, d), F32),
        scratch_shapes=[pltpu.VMEM((A_HEADS, dk, dv), F32),
                        pltpu.VMEM((A_HEADS, SUBLANES, dk), F32),
                        pltpu.VMEM((A_HEADS, SUBLANES, LANES), F32),
                        pltpu.VMEM((ts, nv), BF16)],
        compiler_params=_params(("parallel", "arbitrary")),
        name="mlstm_scan",
    )(q, k, v, og, gr, norm_g.reshape(1, nv), w_out.astype(BF16), h)


def _gla_in_kernel(h_ref, g_ref, wq_ref, wk_ref, wv_ref, wr_ref, wg1_ref, wg2_ref, gb_ref,
                   q_ref, k_ref, v_ref, rg_ref, la_ref):
    dk = wq_ref.shape[1] // B_HEADS
    xn = _rms(h_ref[...], g_ref[...]).astype(BF16)
    q_ref[...] = (_dot(xn, wq_ref[...]) * (dk ** -0.5)).astype(BF16)
    k_ref[...] = _dot(xn, wk_ref[...]).astype(BF16)
    v_ref[...] = _dot(xn, wv_ref[...]).astype(BF16)
    rg_ref[...] = jax.nn.silu(_dot(xn, wr_ref[...])).astype(BF16)
    g_low = _dot(xn, wg1_ref[...]).astype(BF16)
    la_ref[...] = jax.nn.log_sigmoid(_dot(g_low, wg2_ref[...]) + gb_ref[...]) / B_TAU


def _bcast_sublane(x3, idx):
    return jnp.broadcast_to(x3[:, idx:idx + 1, :], x3.shape)


def _boundary_rows(b, half):
    L, w = b.shape
    if half >= SUBLANES:
        nb = L // (2 * half)
        b4 = b.reshape(nb, 2 * half // SUBLANES, SUBLANES, w)
        last = b4[:, half // SUBLANES - 1, SUBLANES - 1:SUBLANES, :]
        return jnp.broadcast_to(last[:, None, :, :], b4.shape).reshape(L, w)
    b3 = b.reshape(L // SUBLANES, SUBLANES, w)
    sub = lax.broadcasted_iota(jnp.int32, b3.shape, 1)
    out = None
    for blk in range(SUBLANES // (2 * half)):
        cand = _bcast_sublane(b3, blk * 2 * half + half - 1)
        out = cand if out is None else jnp.where(sub >= blk * 2 * half, cand, out)
    return out.reshape(L, w)


def _gla_scan_kernel(q_ref, k_ref, v_ref, rg_ref, la_ref, ng_ref, wout_ref, h_ref, o_ref,
                     st_ref, y_ref):
    ts = q_ref.shape[0]
    dk = q_ref.shape[1] // B_HEADS
    dv = v_ref.shape[1] // B_HEADS
    L = MIX_CHUNK

    @pl.when(pl.program_id(1) == 0)
    def _():
        st_ref[...] = jnp.zeros(st_ref.shape, F32)

    row = lax.broadcasted_iota(jnp.int32, (L, L), 0)
    col = lax.broadcasted_iota(jnp.int32, (L, L), 1)
    tri = (col <= row).astype(BF16)
    eye = col == row
    eye_k = (lax.broadcasted_iota(jnp.int32, (dk, dk), 0) == lax.broadcasted_iota(jnp.int32, (dk, dk), 1))
    pos = lax.broadcasted_iota(jnp.int32, (L, dk), 0)

    halves = []
    half = L // 2
    while half >= 1:
        halves.append(half)
        half //= 2

    for c in range(ts // L):
        rows = slice(c * L, (c + 1) * L)
        b_all = _dot_exact_lhs01(tri, la_ref[rows, :])
        for hd in range(B_HEADS):
            kc = slice(hd * dk, (hd + 1) * dk)
            vc = slice(hd * dv, (hd + 1) * dv)
            b = b_all[:, kc]
            qh = q_ref[rows, kc].astype(F32)
            kh = k_ref[rows, kc].astype(F32)
            vh = v_ref[rows, vc]
            s_mat = jnp.where(eye, _dot_nt(qh.astype(BF16), kh.astype(BF16)), 0.0)
            for half in halves:
                ref_b = _boundary_rows(b, half)
                late = (pos & half) != 0
                qt = qh * jnp.exp(jnp.where(late, b - ref_b, -jnp.inf))
                kt = kh * jnp.exp(jnp.where(late, -jnp.inf, ref_b - b))
                same_block = (row // (2 * half)) == (col // (2 * half))
                s_mat = s_mat + jnp.where(same_block, _dot_nt(qt.astype(BF16), kt.astype(BF16)), 0.0)
            state = st_ref[hd]
            o = _dot(s_mat.astype(BF16), vh) + _dot((qh * jnp.exp(b)).astype(BF16), state.astype(BF16))
            b_last = b[L - 1:L, :]
            k_dec = kh * jnp.exp(b_last - b)
            decay_col = jnp.sum(jnp.where(eye_k, jnp.exp(b_last), 0.0), axis=1, keepdims=True)
            st_ref[hd] = decay_col * state + _dot_tn(k_dec.astype(BF16), vh)
            yh = _head_rms(o, ng_ref[:, vc]) * rg_ref[rows, vc].astype(F32)
            y_ref[rows, vc] = yh.astype(BF16)

    o_ref[...] = h_ref[...] + _dot(y_ref[...], wout_ref[...])


def _gla_layer(h, gain, w_in, gate_w2, gate_b, norm_g, w_out, batch):
    rows, d = h.shape
    seq = rows // batch
    nk = gate_w2.shape[1]
    dk = nk // B_HEADS
    dv = d // B_HEADS
    nv = B_HEADS * dv
    wq = w_in[:, :nk].astype(BF16)
    wk = w_in[:, nk:2 * nk].astype(BF16)
    wv = w_in[:, 2 * nk:2 * nk + nv].astype(BF16)
    wr = w_in[:, 2 * nk + nv:2 * nk + 2 * nv].astype(BF16)
    wg1 = jnp.pad(w_in[:, 2 * nk + 2 * nv:], ((0, 0), (0, LANES - B_RANK))).astype(BF16)
    wg2 = jnp.pad(gate_w2, ((0, LANES - B_RANK), (0, 0))).astype(BF16)
    tm = ROW_TILE
    row_tile = lambda i: (i, 0)
    q, k, v, rg, la = pl.pallas_call(
        _gla_in_kernel,
        grid=(rows // tm,),
        in_specs=[pl.BlockSpec((tm, d), row_tile), _whole((1, d)), _whole((d, nk)), _whole((d, nk)),
                  _whole((d, nv)), _whole((d, nv)), _whole((d, LANES)), _whole((LANES, nk)),
                  _whole((1, nk))],
        out_specs=[pl.BlockSpec((tm, nk), row_tile), pl.BlockSpec((tm, nk), row_tile),
                   pl.BlockSpec((tm, nv), row_tile), pl.BlockSpec((tm, nv), row_tile),
                   pl.BlockSpec((tm, nk), row_tile)],
        out_shape=[jax.ShapeDtypeStruct((rows, nk), BF16), jax.ShapeDtypeStruct((rows, nk), BF16),
                   jax.ShapeDtypeStruct((rows, nv), BF16), jax.ShapeDtypeStruct((rows, nv), BF16),
                   jax.ShapeDtypeStruct((rows, nk), F32)],
        compiler_params=_params(("parallel",)),
        name="gla_in",
    )(h, gain, wq, wk, wv, wr, wg1, wg2, gate_b.reshape(1, nk))

    ts = SCAN_TILE
    per_batch = seq // ts
    tile = lambda b, i: (b * per_batch + i, 0)
    return pl.pallas_call(
        _gla_scan_kernel,
        grid=(batch, per_batch),
        in_specs=[pl.BlockSpec((ts, nk), tile), pl.BlockSpec((ts, nk), tile),
                  pl.BlockSpec((ts, nv), tile), pl.BlockSpec((ts, nv), tile),
                  pl.BlockSpec((ts, nk), tile), _whole((1, nv)), _whole((nv, d)),
                  pl.BlockSpec((ts, d), tile)],
        out_specs=pl.BlockSpec((ts, d), tile),
        out_shape=jax.ShapeDtypeStruct((rows, d), F32),
        scratch_shapes=[pltpu.VMEM((B_HEADS, dk, dv), F32), pltpu.VMEM((ts, nv), BF16)],
        compiler_params=_params(("parallel", "arbitrary")),
        name="gla_scan",
    )(q, k, v, rg, la, norm_g.reshape(1, nv), w_out.astype(BF16), h)


def _s5_scan_kernel(h_ref, g_ref, win_ref, perm_ref, permt_ref, bmat_ref, acoef_ref, apow_ref, atile_ref,
                    cmat_ref, dskip_ref, y_ref, carry_ref, bu_ref, s_ref, yp_ref):
    ts, d = h_ref.shape
    nblk = bmat_ref.shape[0]
    half = bmat_ref.shape[2] // 2
    nj = ts // SUBLANES

    @pl.when(pl.program_id(1) == 0)
    def _():
        carry_ref[...] = jnp.zeros(carry_ref.shape, F32)

    xn = _rms(h_ref[...], g_ref[...]).astype(BF16)
    u = _dot(xn, win_ref[...])
    up = _dot(perm_ref[...], u.astype(BF16)).astype(BF16)

    for blk in range(nblk):
        bu_ref[...] = _dot(up[:, blk * LANES:(blk + 1) * LANES], bmat_ref[blk])
        a_re = acoef_ref[blk, 0]
        a_im = acoef_ref[blk, 1]

        def local_step(j, x):
            x_re, x_im = x
            r = pl.ds(pl.multiple_of(j * SUBLANES, SUBLANES), SUBLANES)
            n_re = a_re * x_re - a_im * x_im + bu_ref[r, 0:half]
            n_im = a_re * x_im + a_im * x_re + bu_ref[r, half:2 * half]
            s_ref[r, 0:half] = n_re
            s_ref[r, half:2 * half] = n_im
            return n_re, n_im

        zero = jnp.zeros((SUBLANES, half), F32)
        end_re, end_im = lax.fori_loop(0, nj, local_step, (zero, zero))

        t_re = atile_ref[blk, 0]
        t_im = atile_ref[blk, 1]
        c_re = carry_ref[blk, 0:1, 0:half]
        c_im = carry_ref[blk, 0:1, half:2 * half]
        cin_re, cin_im = [], []
        for r in range(SUBLANES):
            cin_re.append(c_re)
            cin_im.append(c_im)
            e_re = end_re[r:r + 1, :] + t_re * c_re - t_im * c_im
            e_im = end_im[r:r + 1, :] + t_re * c_im + t_im * c_re
            c_re, c_im = e_re, e_im
        carry_ref[blk, :, 0:half] = jnp.broadcast_to(c_re, (SUBLANES, half))
        carry_ref[blk, :, half:2 * half] = jnp.broadcast_to(c_im, (SUBLANES, half))
        cin_re = jnp.concatenate(cin_re, axis=0)
        cin_im = jnp.concatenate(cin_im, axis=0)

        def fix_step(j, _):
            r = pl.ds(pl.multiple_of(j * SUBLANES, SUBLANES), SUBLANES)
            p_re = apow_ref[blk, 0, pl.ds(j, 1), :]
            p_im = apow_ref[blk, 1, pl.ds(j, 1), :]
            f_re = s_ref[r, 0:half] + p_re * cin_re - p_im * cin_im
            f_im = s_ref[r, half:2 * half] + p_re * cin_im + p_im * cin_re
            s_ref[r, 0:half] = f_re
            s_ref[r, half:2 * half] = f_im
            return 0

        lax.fori_loop(0, nj, fix_step, 0)
        yp_ref[:, blk * LANES:(blk + 1) * LANES] = _dot(s_ref[...].astype(BF16), cmat_ref[blk])

    y = _dot_exact_lhs01(permt_ref[...], yp_ref[...]) + dskip_ref[...] * u
    y_ref[...] = jax.nn.gelu(y).astype(BF16)


def _s5_out_kernel(y_ref, wg_ref, bg_ref, wo_ref, h_ref, o_ref):
    yb = y_ref[...]
    z = yb.astype(F32) * jax.nn.sigmoid(_dot(yb, wg_ref[...]) + bg_ref[...])
    o_ref[...] = h_ref[...] + _dot(z.astype(BF16), wo_ref[...])


def _s5_discretise(lam_re, lam_im, log_dt, b_re, b_im, c_re, c_im, ts):
    groups, states = lam_re.shape
    gpb = S5_BLOCK_GROUPS
    nblk = groups // gpb
    lr = lam_re.astype(F32)
    li = lam_im.astype(F32)
    dt = jnp.exp(log_dt.astype(F32))[:, None]
    mag = jnp.exp(lr * dt)
    ab_re = mag * jnp.cos(li * dt)
    ab_im = mag * jnp.sin(li * dt)
    den = lr * lr + li * li
    z_re = ((ab_re - 1.0) * lr + ab_im * li) / den
    z_im = (ab_im * lr - (ab_re - 1.0) * li) / den
    br = b_re.astype(F32)
    bi = b_im.astype(F32)
    bb_re = z_re[..., None] * br - z_im[..., None] * bi
    bb_im = z_re[..., None] * bi + z_im[..., None] * br
    eye = jnp.eye(gpb, dtype=F32)

    def in_block(bb):
        t = bb.reshape(nblk, gpb, states, C_GROUP)
        return jnp.einsum('ngph,gk->nghkp', t, eye).reshape(nblk, gpb * C_GROUP, gpb * states)

    bmat = jnp.concatenate([in_block(bb_re), in_block(bb_im)], axis=-1).astype(BF16)

    def out_block(cc):
        t = cc.astype(F32).reshape(nblk, gpb, C_GROUP, states)
        return jnp.einsum('nghp,gk->ngpkh', t, eye).reshape(nblk, gpb * states, gpb * C_GROUP)

    cmat = jnp.concatenate([out_block(c_re), -out_block(c_im)], axis=1).astype(BF16)

    def powers(n):
        n = n.astype(F32)[:, None, None]
        m = jnp.exp(n * (lr * dt)[None])
        ang = n * (li * dt)[None]
        pr = (m * jnp.cos(ang)).reshape(-1, nblk, gpb * states).transpose(1, 0, 2)
        pi = (m * jnp.sin(ang)).reshape(-1, nblk, gpb * states).transpose(1, 0, 2)
        return jnp.stack([pr, pi], axis=1)

    nj = ts // SUBLANES
    acoef = jnp.broadcast_to(powers(jnp.ones((1,)))[:, :, 0:1, :], (nblk, 2, SUBLANES, gpb * states))
    apow = powers(jnp.arange(1, nj + 1))
    atile = powers(jnp.full((1,), nj))
    return bmat, cmat, acoef, apow, atile


def _s5_layer(h, gain, w_in, lam_re, lam_im, log_dt, b_re, b_im, c_re, c_im, d_skip, w_gate, b_gate,
              w_out, batch):
    rows, d = h.shape
    seq = rows // batch
    ts = SCAN_TILE
    nj = ts // SUBLANES
    per_batch = seq // ts
    bmat, cmat, acoef, apow, atile = _s5_discretise(lam_re, lam_im, log_dt, b_re, b_im, c_re, c_im, ts)
    nblk, _, sw = bmat.shape
    p = jnp.arange(ts)
    perm = (p[None, :] == ((p % SUBLANES) * nj + p // SUBLANES)[:, None]).astype(BF16)
    tile = lambda b, i: (b * per_batch + i, 0)
    y = pl.pallas_call(
        _s5_scan_kernel,
        grid=(batch, per_batch),
        in_specs=[pl.BlockSpec((ts, d), tile), _whole((1, d)), _whole((d, d)), _whole((ts, ts)), _whole((ts, ts)),
                  _whole(bmat.shape), _whole(acoef.shape), _whole(apow.shape), _whole(atile.shape),
                  _whole(cmat.shape), _whole((1, d))],
        out_specs=pl.BlockSpec((ts, d), tile),
        out_shape=jax.ShapeDtypeStruct((rows, d), BF16),
        scratch_shapes=[pltpu.VMEM((nblk, SUBLANES, sw), F32), pltpu.VMEM((ts, sw), F32),
                        pltpu.VMEM((ts, sw), F32),
                        pltpu.VMEM((ts, d), F32)],
        compiler_params=_params(("parallel", "arbitrary")),
        name="s5_scan",
    )(h, gain, w_in.astype(BF16), perm, perm.T, bmat, acoef, apow, atile, cmat, d_skip.reshape(1, d))

    tm = ROW_TILE
    row_tile = lambda i: (i, 0)
    return pl.pallas_call(
        _s5_out_kernel,
        grid=(rows // tm,),
        in_specs=[pl.BlockSpec((tm, d), row_tile), _whole((d, d)), _whole((1, d)), _whole((d, d)),
                  pl.BlockSpec((tm, d), row_tile)],
        out_specs=pl.BlockSpec((tm, d), row_tile),
        out_shape=jax.ShapeDtypeStruct((rows, d), F32),
        compiler_params=_params(("parallel",)),
        name="s5_out",
    )(y, w_gate.astype(BF16), b_gate.reshape(1, d), w_out.astype(BF16), h)


def _ff_chunks(w, axis):
    if axis == 1:
        k, n = w.shape
        return w.reshape(k, n // FF_CHUNK, FF_CHUNK).transpose(1, 0, 2).astype(BF16)
    n, k = w.shape
    return w.reshape(n // FF_CHUNK, FF_CHUNK, k).astype(BF16)


def kernel(x, mem, norm_mix, norm_x, norm_ffn, norm_mem, norm_final, a_w_in, a_conv_w, a_gate_b, a_norm, a_w_out, b_w_in, b_gate_w2, b_gate_b, b_norm, b_w_out, c_w_in, c_lam_re, c_lam_im, c_log_dt, c_b_re, c_b_im, c_c_re, c_c_im, c_d, c_w_gate, c_b_gate, c_w_out, x_w_q, x_w_kv, x_w_o, f_w_gate, f_w_up, f_w_down):
    batch, seq, d = x.shape
    mem_len = mem.shape[1]
    depth = norm_mix.shape[0]
    assert seq % ROW_TILE == 0 and seq % SCAN_TILE == 0 and SCAN_TILE % MIX_CHUNK == 0
    assert f_w_gate.shape[2] % FF_CHUNK == 0 and d % (X_HEADS * LANES) == 0

    w_kv_all = jnp.transpose(x_w_kv, (1, 0, 2)).reshape(d, depth * 2 * d).astype(BF16)
    kv = _rms_matmul(mem.reshape(batch * mem_len, d), norm_mem.reshape(1, d), w_kv_all, mem_len, BF16)

    h = x.reshape(batch * seq, d)
    for i in range(depth):
        kind = i % N_MIXERS
        j = i // N_MIXERS
        gain = norm_mix[i].reshape(1, d)
        if kind == 0:
            h = _mlstm_layer(h, gain, a_w_in[j], a_conv_w[j], a_gate_b[j], a_norm[j], a_w_out[j], batch)
        elif kind == 1:
            h = _gla_layer(h, gain, b_w_in[j], b_gate_w2[j], b_gate_b[j], b_norm[j], b_w_out[j], batch)
        else:
            h = _s5_layer(h, gain, c_w_in[j], c_lam_re[j], c_lam_im[j], c_log_dt[j], c_b_re[j], c_b_im[j],
                          c_c_re[j], c_c_im[j], c_d[j], c_w_gate[j], c_b_gate[j], c_w_out[j], batch)
        h = _xattn(h, norm_x[i].reshape(1, d), x_w_q[i].astype(BF16), kv, i, x_w_o[i].astype(BF16),
                   batch, mem_len)
        h = _swiglu(h, norm_ffn[i].reshape(1, d), _ff_chunks(f_w_gate[i], 1), _ff_chunks(f_w_up[i], 1),
                    _ff_chunks(f_w_down[i], 0), norm_final.reshape(1, d), final_norm=(i == depth - 1))
    return h.reshape(batch, seq, d)
```

```python
import functools
import math

import jax
import jax.numpy as jnp
from jax import lax
from jax.experimental import pallas as pl
from jax.experimental.pallas import tpu as pltpu

F32 = jnp.float32
BF16 = jnp.bfloat16
EPS = 1e-6

N_MIXERS = 3
A_HEADS = 4
A_CONV = 4
B_HEADS = 4
B_RANK = 16
B_TAU = 16.0
C_GROUP = 16
C_STATE = 64
X_HEADS = 4

LANES = 128
SUBLANES = 8
VMEM_LIMIT_BYTES = 56 * 1024 * 1024

ROW_TILE = 512
MLSTM_TILE = 256
GLA_TILE = 512
MIX_CHUNK = 128
S5_TILE = 256
S5_UNROLL = 32
FF_CHUNK = 256
S5_BLOCK_GROUPS = LANES // C_GROUP


def _params(semantics):
    return pltpu.CompilerParams(dimension_semantics=semantics, vmem_limit_bytes=VMEM_LIMIT_BYTES)


def _whole(shape):
    zeros = (0,) * len(shape)
    return pl.BlockSpec(shape, lambda *_: zeros)


def _rms(x, g):
    return x * lax.rsqrt(jnp.mean(x * x, axis=-1, keepdims=True) + EPS) * g


def _dot(a, b):
    return jnp.dot(a, b, preferred_element_type=F32)


def _dot_nt(a, b):
    return lax.dot_general(a, b, (((1,), (1,)), ((), ())), preferred_element_type=F32)


def _dot_tn(a, b):
    return lax.dot_general(a, b, (((0,), (0,)), ((), ())), preferred_element_type=F32)


def _split3(x):
    hi = x.astype(BF16)
    r = x - hi.astype(F32)
    mid = r.astype(BF16)
    lo = (r - mid.astype(F32)).astype(BF16)
    return hi, mid, lo


def _dot_exact_lhs01(m01, x):
    hi, mid, lo = _split3(x)
    return _dot(m01, hi) + _dot(m01, mid) + _dot(m01, lo)


def _head_rms(o, g):
    return o * lax.rsqrt(jnp.mean(o * o, axis=-1, keepdims=True) + EPS) * g


def _mem_kv_kernel(x_ref, g_ref, w_ref, o_ref):
    xn = _rms(x_ref[...], g_ref[...]).astype(BF16)
    o_ref[...] = _dot(xn, w_ref[0]).astype(BF16)


def _mem_kv(x, g, w, tm):
    rows, d = x.shape
    depth, _, n = w.shape
    return pl.pallas_call(
        _mem_kv_kernel,
        grid=(depth, rows // tm),
        in_specs=[pl.BlockSpec((tm, d), lambda l, i: (i, 0)), _whole((1, d)),
                  pl.BlockSpec((1, d, n), lambda l, i: (l, 0, 0))],
        out_specs=pl.BlockSpec((tm, n), lambda l, i: (i, l)),
        out_shape=jax.ShapeDtypeStruct((rows, depth * n), BF16),
        compiler_params=_params(("parallel", "parallel")),
        name="mem_kv_proj",
    )(x, g, w)


def _xattn_kernel(h_ref, g_ref, wq_ref, kv_ref, wo_ref, o_ref):
    d = h_ref.shape[1]
    dh = d // X_HEADS
    x = h_ref[...]
    xn = _rms(x, g_ref[...]).astype(BF16)
    q = _dot(xn, wq_ref[...]).astype(BF16)
    outs = []
    for hd in range(X_HEADS):
        qh = q[:, hd * dh:(hd + 1) * dh]
        kh = kv_ref[:, hd * dh:(hd + 1) * dh]
        vh = kv_ref[:, d + hd * dh:d + (hd + 1) * dh]
        s = _dot_nt(qh, kh) * (dh ** -0.5)
        e = jnp.exp(s - jnp.max(s, axis=-1, keepdims=True))
        p = e / jnp.sum(e, axis=-1, keepdims=True)
        outs.append(_dot(p.astype(BF16), vh))
    o = jnp.concatenate(outs, axis=-1).astype(BF16)
    o_ref[...] = x + _dot(o, wo_ref[...])


def _xattn(h, g, wq, kv, layer, wo, batch, mem_len):
    rows, d = h.shape
    tm = ROW_TILE
    per_batch = rows // batch // tm
    return pl.pallas_call(
        _xattn_kernel,
        grid=(batch, per_batch),
        in_specs=[
            pl.BlockSpec((tm, d), lambda b, i: (b * per_batch + i, 0)),
            _whole((1, d)),
            _whole((d, d)),
            pl.BlockSpec((mem_len, 2 * d), lambda b, i: (b, layer)),
            _whole((d, d)),
        ],
        out_specs=pl.BlockSpec((tm, d), lambda b, i: (b * per_batch + i, 0)),
        out_shape=jax.ShapeDtypeStruct((rows, d), F32),
        compiler_params=_params(("parallel", "parallel")),
        name="xattn",
    )(h, g, wq, kv, wo)


def _swiglu_kernel(h_ref, g_ref, wg_ref, wu_ref, wd_ref, gf_ref, o_ref, *, final_norm):
    x = h_ref[...]
    xn = _rms(x, g_ref[...]).astype(BF16)
    o_ref[...] = x
    for c in range(wg_ref.shape[1] // FF_CHUNK):
        cols = slice(c * FF_CHUNK, (c + 1) * FF_CHUNK)
        gate = _dot(xn, wg_ref[:, cols])
        up = _dot(xn, wu_ref[:, cols])
        act = (jax.nn.silu(gate) * up).astype(BF16)
        o_ref[...] += _dot(act, wd_ref[cols, :])
    if final_norm:
        o_ref[...] = _rms(o_ref[...], gf_ref[...])


def _swiglu(h, g, wg, wu, wd, gf, final_norm):
    rows, d = h.shape
    tm = ROW_TILE
    return pl.pallas_call(
        functools.partial(_swiglu_kernel, final_norm=final_norm),
        grid=(rows // tm,),
        in_specs=[pl.BlockSpec((tm, d), lambda i: (i, 0)), _whole((1, d)), _whole(wg.shape),
                  _whole(wu.shape), _whole(wd.shape), _whole((1, d))],
        out_specs=pl.BlockSpec((tm, d), lambda i: (i, 0)),
        out_shape=jax.ShapeDtypeStruct((rows, d), F32),
        compiler_params=_params(("parallel",)),
        name="swiglu",
    )(h, g, wg, wu, wd, gf)


def _mlstm_in_kernel(h_ref, g_ref, wqk_ref, wv_ref, wo_ref, wgt_ref, cw_ref, gb_ref,
                     q_ref, k_ref, v_ref, og_ref, gr_ref, xp_ref, *, tiles_per_seq):
    tm, d = h_ref.shape
    nqk = wqk_ref.shape[1]
    dk = nqk // (2 * A_HEADS)
    xn = _rms(h_ref[...], g_ref[...]).astype(BF16)
    qk = _dot(xn, wqk_ref[...])

    @pl.when(pl.program_id(0) % tiles_per_seq == 0)
    def _():
        xp_ref[0:SUBLANES, :] = jnp.zeros((SUBLANES, nqk), F32)

    xp_ref[SUBLANES:SUBLANES + tm, :] = qk
    cw = cw_ref[...]
    y = cw[A_CONV - 1:A_CONV, :] * qk
    for j in range(A_CONV - 1):
        y = y + cw[j:j + 1, :] * xp_ref[pl.ds(SUBLANES - (A_CONV - 1) + j, tm), :]
    xp_ref[0:SUBLANES, :] = qk[tm - SUBLANES:tm, :]
    y = jax.nn.silu(y)
    q_ref[...] = (y[:, :nqk // 2] * (dk ** -0.5)).astype(BF16)
    k_ref[...] = y[:, nqk // 2:].astype(BF16)
    v_ref[...] = _dot(xn, wv_ref[...]).astype(BF16)
    og_ref[...] = jax.nn.sigmoid(_dot(xn, wo_ref[...])).astype(BF16)
    gr = _dot_nt(wgt_ref[...], xn) + gb_ref[...]
    is_input_gate = lax.broadcasted_iota(jnp.int32, gr.shape, 0) < A_HEADS
    gr_ref[...] = jnp.where(is_input_gate, gr, jax.nn.log_sigmoid(gr))


def _mlstm_scan_kernel(q_ref, k_ref, v_ref, og_ref, gr_ref, ng_ref, wout_ref, h_ref, o_ref,
                       c_ref, n_ref, m_ref, y_ref):
    ts = q_ref.shape[0]
    dk = q_ref.shape[1] // A_HEADS
    dv = v_ref.shape[1] // A_HEADS
    L = MIX_CHUNK

    @pl.when(pl.program_id(1) == 0)
    def _():
        c_ref[...] = jnp.zeros(c_ref.shape, F32)
        n_ref[...] = jnp.zeros(n_ref.shape, F32)
        m_ref[...] = jnp.zeros(m_ref.shape, F32)

    row = lax.broadcasted_iota(jnp.int32, (L, L), 0)
    col = lax.broadcasted_iota(jnp.int32, (L, L), 1)
    causal = col <= row
    eye = col == row

    def to_col(x_row):
        return jnp.sum(jnp.where(eye, x_row, 0.0), axis=1, keepdims=True)

    for c in range(ts // L):
        rows = slice(c * L, (c + 1) * L)
        for hd in range(A_HEADS):
            i_row = gr_ref[hd:hd + 1, rows]
            f_row = gr_ref[A_HEADS + hd:A_HEADS + hd + 1, rows]
            b_col = jnp.sum(jnp.where(causal, f_row, 0.0), axis=1, keepdims=True)
            b_row = jnp.sum(jnp.where(eye, b_col, 0.0), axis=0, keepdims=True)
            i_col = to_col(i_row)
            m_prev = m_ref[hd][0:1, 0:1]
            dmat = jnp.where(causal, b_col - b_row + i_row, -jnp.inf)
            inter = b_col + m_prev
            m_t = jnp.maximum(inter, jnp.max(dmat, axis=1, keepdims=True))
            w_intra = jnp.exp(dmat - m_t)
            w_inter = jnp.exp(inter - m_t)
            qh = q_ref[rows, hd * dk:(hd + 1) * dk]
            kh = k_ref[rows, hd * dk:(hd + 1) * dk]
            vh = v_ref[rows, hd * dv:(hd + 1) * dv]
            c_prev = c_ref[hd]
            n_prev = n_ref[hd][0:1, :]
            s = _dot_nt(qh, kh) * w_intra
            num = _dot(s.astype(BF16), vh) + w_inter * _dot(qh, c_prev.astype(BF16))
            qn = jnp.sum(qh.astype(F32) * n_prev, axis=1, keepdims=True)
            den = jnp.sum(s, axis=1, keepdims=True) + w_inter * qn
            hh = num / jnp.maximum(jnp.abs(den), jnp.exp(-m_t))
            g_tot = jnp.sum(f_row, axis=1, keepdims=True)
            a_col = g_tot - b_col + i_col
            m_new = jnp.maximum(g_tot + m_prev, jnp.max(a_col, axis=0, keepdims=True))
            decay = jnp.exp(g_tot + m_prev - m_new)
            kw = kh.astype(F32) * jnp.exp(a_col - m_new)
            c_ref[hd] = decay * c_prev + _dot_tn(kw.astype(BF16), vh)
            n_new = decay * n_prev + jnp.sum(kw, axis=0, keepdims=True)
            n_ref[hd] = jnp.broadcast_to(n_new, n_ref.shape[1:])
            m_ref[hd] = jnp.broadcast_to(m_new, m_ref.shape[1:])
            cols = slice(hd * dv, (hd + 1) * dv)
            yh = _head_rms(hh, ng_ref[:, cols]) * og_ref[rows, cols].astype(F32)
            y_ref[rows, cols] = yh.astype(BF16)

    o_ref[...] = h_ref[...] + _dot(y_ref[...], wout_ref[...])


def _mlstm_layer(h, gain, w_in, conv_w, gate_b, norm_g, w_out, batch):
    rows, d = h.shape
    seq = rows // batch
    nqk = conv_w.shape[1]
    dk = nqk // (2 * A_HEADS)
    dv = d // A_HEADS
    nv = A_HEADS * dv
    ng = 2 * A_HEADS
    wqk = w_in[:, :nqk].astype(BF16)
    wv = w_in[:, nqk:nqk + nv].astype(BF16)
    wo = w_in[:, nqk + nv:nqk + 2 * nv].astype(BF16)
    wgt = w_in[:, nqk + 2 * nv:].T.astype(BF16)
    tm = ROW_TILE
    q, k, v, og, gr = pl.pallas_call(
        functools.partial(_mlstm_in_kernel, tiles_per_seq=seq // tm),
        grid=(rows // tm,),
        in_specs=[pl.BlockSpec((tm, d), lambda i: (i, 0)), _whole((1, d)), _whole((d, nqk)),
                  _whole((d, nv)), _whole((d, nv)), _whole((ng, d)), _whole((A_CONV, nqk)),
                  _whole((ng, 1))],
        out_specs=[pl.BlockSpec((tm, nqk // 2), lambda i: (i, 0)),
                   pl.BlockSpec((tm, nqk // 2), lambda i: (i, 0)),
                   pl.BlockSpec((tm, nv), lambda i: (i, 0)),
                   pl.BlockSpec((tm, nv), lambda i: (i, 0)),
                   pl.BlockSpec((ng, tm), lambda i: (0, i))],
        out_shape=[jax.ShapeDtypeStruct((rows, nqk // 2), BF16),
                   jax.ShapeDtypeStruct((rows, nqk // 2), BF16),
                   jax.ShapeDtypeStruct((rows, nv), BF16),
                   jax.ShapeDtypeStruct((rows, nv), BF16),
                   jax.ShapeDtypeStruct((ng, rows), F32)],
        scratch_shapes=[pltpu.VMEM((tm + SUBLANES, nqk), F32)],
        compiler_params=_params(("arbitrary",)),
        name="mlstm_in",
    )(h, gain, wqk, wv, wo, wgt, conv_w, gate_b.reshape(ng, 1))

    ts = MLSTM_TILE
    per_batch = seq // ts
    tile = lambda b, i: (b * per_batch + i, 0)
    return pl.pallas_call(
        _mlstm_scan_kernel,
        grid=(batch, per_batch),
        in_specs=[pl.BlockSpec((ts, nqk // 2), tile), pl.BlockSpec((ts, nqk // 2), tile),
                  pl.BlockSpec((ts, nv), tile), pl.BlockSpec((ts, nv), tile),
                  pl.BlockSpec((ng, ts), lambda b, i: (0, b * per_batch + i)),
                  _whole((1, nv)), _whole((nv, d)), pl.BlockSpec((ts, d), tile)],
        out_specs=pl.BlockSpec((ts, d), tile),
        out_shape=jax.ShapeDtypeStruct((rows, d), F32),
        scratch_shapes=[pltpu.VMEM((A_HEADS, dk, dv), F32),
                        pltpu.VMEM((A_HEADS, SUBLANES, dk), F32),
                        pltpu.VMEM((A_HEADS, SUBLANES, LANES), F32),
                        pltpu.VMEM((ts, nv), BF16)],
        compiler_params=_params(("parallel", "arbitrary")),
        name="mlstm_scan",
    )(q, k, v, og, gr, norm_g.reshape(1, nv), w_out.astype(BF16), h)


def _gla_in_kernel(h_ref, g_ref, wq_ref, wk_ref, wv_ref, wr_ref, wg1_ref, wg2_ref, gb_ref,
                   q_ref, k_ref, v_ref, rg_ref, la_ref):
    dk = wq_ref.shape[1] // B_HEADS
    xn = _rms(h_ref[...], g_ref[...]).astype(BF16)
    q_ref[...] = (_dot(xn, wq_ref[...]) * (dk ** -0.5)).astype(BF16)
    k_ref[...] = _dot(xn, wk_ref[...]).astype(BF16)
    v_ref[...] = _dot(xn, wv_ref[...]).astype(BF16)
    rg_ref[...] = jax.nn.silu(_dot(xn, wr_ref[...])).astype(BF16)
    g_low = _dot(xn, wg1_ref[...]).astype(BF16)
    la_ref[...] = jax.nn.log_sigmoid(_dot(g_low, wg2_ref[...]) + gb_ref[...]) / B_TAU


def _bcast_sublane(x3, idx):
    return jnp.broadcast_to(x3[:, idx:idx + 1, :], x3.shape)


def _boundary_rows(b, half):
    L, w = b.shape
    if half >= SUBLANES:
        nb = L // (2 * half)
        b4 = b.reshape(nb, 2 * half // SUBLANES, SUBLANES, w)
        last = b4[:, half // SUBLANES - 1, SUBLANES - 1:SUBLANES, :]
        return jnp.broadcast_to(last[:, None, :, :], b4.shape).reshape(L, w)
    b3 = b.reshape(L // SUBLANES, SUBLANES, w)
    sub = lax.broadcasted_iota(jnp.int32, b3.shape, 1)
    out = None
    for blk in range(SUBLANES // (2 * half)):
        cand = _bcast_sublane(b3, blk * 2 * half + half - 1)
        out = cand if out is None else jnp.where(sub >= blk * 2 * half, cand, out)
    return out.reshape(L, w)


def _gla_scan_kernel(q_ref, k_ref, v_ref, rg_ref, la_ref, ng_ref, wout_ref, h_ref, o_ref,
                     st_ref, y_ref):
    ts = q_ref.shape[0]
    dk = q_ref.shape[1] // B_HEADS
    dv = v_ref.shape[1] // B_HEADS
    L = MIX_CHUNK

    @pl.when(pl.program_id(1) == 0)
    def _():
        st_ref[...] = jnp.zeros(st_ref.shape, F32)

    row = lax.broadcasted_iota(jnp.int32, (L, L), 0)
    col = lax.broadcasted_iota(jnp.int32, (L, L), 1)
    tri = (col <= row).astype(BF16)
    eye = col == row
    eye_k = (lax.broadcasted_iota(jnp.int32, (dk, dk), 0) == lax.broadcasted_iota(jnp.int32, (dk, dk), 1))
    pos = lax.broadcasted_iota(jnp.int32, (L, dk), 0)

    halves = []
    half = L // 2
    while half >= 1:
        halves.append(half)
        half //= 2

    for c in range(ts // L):
        rows = slice(c * L, (c + 1) * L)
        b_all = _dot_exact_lhs01(tri, la_ref[rows, :])
        for hd in range(B_HEADS):
            kc = slice(hd * dk, (hd + 1) * dk)
            vc = slice(hd * dv, (hd + 1) * dv)
            b = b_all[:, kc]
            qh = q_ref[rows, kc].astype(F32)
            kh = k_ref[rows, kc].astype(F32)
            vh = v_ref[rows, vc]
            s_mat = jnp.where(eye, _dot_nt(qh.astype(BF16), kh.astype(BF16)), 0.0)
            for half in halves:
                ref_b = _boundary_rows(b, half)
                late = (pos & half) != 0
                qt = qh * jnp.exp(jnp.where(late, b - ref_b, -jnp.inf))
                kt = kh * jnp.exp(jnp.where(late, -jnp.inf, ref_b - b))
                same_block = (row // (2 * half)) == (col // (2 * half))
                s_mat = s_mat + jnp.where(same_block, _dot_nt(qt.astype(BF16), kt.astype(BF16)), 0.0)
            state = st_ref[hd]
            o = _dot(s_mat.astype(BF16), vh) + _dot((qh * jnp.exp(b)).astype(BF16), state.astype(BF16))
            b_last = b[L - 1:L, :]
            k_dec = kh * jnp.exp(b_last - b)
            decay_col = jnp.sum(jnp.where(eye_k, jnp.exp(b_last), 0.0), axis=1, keepdims=True)
            st_ref[hd] = decay_col * state + _dot_tn(k_dec.astype(BF16), vh)
            yh = _head_rms(o, ng_ref[:, vc]) * rg_ref[rows, vc].astype(F32)
            y_ref[rows, vc] = yh.astype(BF16)

    o_ref[...] = h_ref[...] + _dot(y_ref[...], wout_ref[...])


def _gla_layer(h, gain, w_in, gate_w2, gate_b, norm_g, w_out, batch):
    rows, d = h.shape
    seq = rows // batch
    nk = gate_w2.shape[1]
    dk = nk // B_HEADS
    dv = d // B_HEADS
    nv = B_HEADS * dv
    wq = w_in[:, :nk].astype(BF16)
    wk = w_in[:, nk:2 * nk].astype(BF16)
    wv = w_in[:, 2 * nk:2 * nk + nv].astype(BF16)
    wr = w_in[:, 2 * nk + nv:2 * nk + 2 * nv].astype(BF16)
    wg1 = jnp.pad(w_in[:, 2 * nk + 2 * nv:], ((0, 0), (0, LANES - B_RANK))).astype(BF16)
    wg2 = jnp.pad(gate_w2, ((0, LANES - B_RANK), (0, 0))).astype(BF16)
    tm = ROW_TILE
    row_tile = lambda i: (i, 0)
    q, k, v, rg, la = pl.pallas_call(
        _gla_in_kernel,
        grid=(rows // tm,),
        in_specs=[pl.BlockSpec((tm, d), row_tile), _whole((1, d)), _whole((d, nk)), _whole((d, nk)),
                  _whole((d, nv)), _whole((d, nv)), _whole((d, LANES)), _whole((LANES, nk)),
                  _whole((1, nk))],
        out_specs=[pl.BlockSpec((tm, nk), row_tile), pl.BlockSpec((tm, nk), row_tile),
                   pl.BlockSpec((tm, nv), row_tile), pl.BlockSpec((tm, nv), row_tile),
                   pl.BlockSpec((tm, nk), row_tile)],
        out_shape=[jax.ShapeDtypeStruct((rows, nk), BF16), jax.ShapeDtypeStruct((rows, nk), BF16),
                   jax.ShapeDtypeStruct((rows, nv), BF16), jax.ShapeDtypeStruct((rows, nv), BF16),
                   jax.ShapeDtypeStruct((rows, nk), F32)],
        compiler_params=_params(("parallel",)),
        name="gla_in",
    )(h, gain, wq, wk, wv, wr, wg1, wg2, gate_b.reshape(1, nk))

    ts = GLA_TILE
    per_batch = seq // ts
    tile = lambda b, i: (b * per_batch + i, 0)
    return pl.pallas_call(
        _gla_scan_kernel,
        grid=(batch, per_batch),
        in_specs=[pl.BlockSpec((ts, nk), tile), pl.BlockSpec((ts, nk), tile),
                  pl.BlockSpec((ts, nv), tile), pl.BlockSpec((ts, nv), tile),
                  pl.BlockSpec((ts, nk), tile), _whole((1, nv)), _whole((nv, d)),
                  pl.BlockSpec((ts, d), tile)],
        out_specs=pl.BlockSpec((ts, d), tile),
        out_shape=jax.ShapeDtypeStruct((rows, d), F32),
        scratch_shapes=[pltpu.VMEM((B_HEADS, dk, dv), F32), pltpu.VMEM((ts, nv), BF16)],
        compiler_params=_params(("parallel", "arbitrary")),
        name="gla_scan",
    )(q, k, v, rg, la, norm_g.reshape(1, nv), w_out.astype(BF16), h)


def _s5_kernel(h_ref, g_ref, win_ref, perm_ref, permt_ref, bmat_ref, acoef_ref, apow_ref, atile_ref,
               cmat_ref, dskip_ref, wg_ref, bg_ref, wo_ref, o_ref, carry_ref, bu_ref, s_ref, sb_ref, yp_ref):
    ts, d = h_ref.shape
    nblk = bmat_ref.shape[0]
    half = bmat_ref.shape[2] // 2
    nj = ts // SUBLANES

    @pl.when(pl.program_id(1) == 0)
    def _():
        carry_ref[...] = jnp.zeros(carry_ref.shape, F32)

    xn = _rms(h_ref[...], g_ref[...]).astype(BF16)
    xp = _dot(perm_ref[...], xn).astype(BF16)
    u = _dot(xp, win_ref[...])
    ub = u.astype(BF16)

    for blk in range(nblk):
        bu_ref[...] = _dot(ub[:, blk * LANES:(blk + 1) * LANES], bmat_ref[blk])
        a_re = acoef_ref[blk, 0]
        a_im = acoef_ref[blk, 1]

        def local_step(j, x):
            x_re, x_im = x
            r = pl.ds(pl.multiple_of(j * SUBLANES, SUBLANES), SUBLANES)
            n_re = a_re * x_re - a_im * x_im + bu_ref[r, 0:half]
            n_im = a_re * x_im + a_im * x_re + bu_ref[r, half:2 * half]
            s_ref[r, 0:half] = n_re
            s_ref[r, half:2 * half] = n_im
            return n_re, n_im

        zero = jnp.zeros((SUBLANES, half), F32)
        end_re, end_im = lax.fori_loop(0, nj, local_step, (zero, zero), unroll=S5_UNROLL)

        t_re = atile_ref[blk, 0]
        t_im = atile_ref[blk, 1]
        c_re = carry_ref[blk, 0:1, 0:half]
        c_im = carry_ref[blk, 0:1, half:2 * half]
        cin_re, cin_im = [], []
        for r in range(SUBLANES):
            cin_re.append(c_re)
            cin_im.append(c_im)
            e_re = end_re[r:r + 1, :] + t_re * c_re - t_im * c_im
            e_im = end_im[r:r + 1, :] + t_re * c_im + t_im * c_re
            c_re, c_im = e_re, e_im
        carry_ref[blk, :, 0:half] = jnp.broadcast_to(c_re, (SUBLANES, half))
        carry_ref[blk, :, half:2 * half] = jnp.broadcast_to(c_im, (SUBLANES, half))
        cin_re = jnp.concatenate(cin_re, axis=0)
        cin_im = jnp.concatenate(cin_im, axis=0)

        def fix_step(jj, _):
            out_re, out_im = [], []
            for k in range(2):
                j = 2 * jj + k
                r = pl.ds(pl.multiple_of(j * SUBLANES, SUBLANES), SUBLANES)
                p_re = apow_ref[blk, 0, pl.ds(j, 1), :]
                p_im = apow_ref[blk, 1, pl.ds(j, 1), :]
                out_re.append(s_ref[r, 0:half] + p_re * cin_re - p_im * cin_im)
                out_im.append(s_ref[r, half:2 * half] + p_re * cin_im + p_im * cin_re)
            r2 = pl.ds(pl.multiple_of(jj * 2 * SUBLANES, 2 * SUBLANES), 2 * SUBLANES)
            sb_ref[r2, 0:half] = jnp.concatenate(out_re, axis=0).astype(BF16)
            sb_ref[r2, half:2 * half] = jnp.concatenate(out_im, axis=0).astype(BF16)
            return 0

        lax.fori_loop(0, nj // 2, fix_step, 0, unroll=S5_UNROLL // 2)
        yp_ref[:, blk * LANES:(blk + 1) * LANES] = _dot(sb_ref[...], cmat_ref[blk])

    y = jax.nn.gelu(yp_ref[...] + dskip_ref[...] * u)
    z = y * jax.nn.sigmoid(_dot(y.astype(BF16), wg_ref[...]) + bg_ref[...])
    z = _dot(permt_ref[...], z.astype(BF16)).astype(BF16)
    o_ref[...] = h_ref[...] + _dot(z, wo_ref[...])


def _s5_discretise(lam_re, lam_im, log_dt, b_re, b_im, c_re, c_im, ts):
    groups, states = lam_re.shape
    gpb = S5_BLOCK_GROUPS
    nblk = groups // gpb
    lr = lam_re.astype(F32)
    li = lam_im.astype(F32)
    dt = jnp.exp(log_dt.astype(F32))[:, None]
    mag = jnp.exp(lr * dt)
    ab_re = mag * jnp.cos(li * dt)
    ab_im = mag * jnp.sin(li * dt)
    den = lr * lr + li * li
    z_re = ((ab_re - 1.0) * lr + ab_im * li) / den
    z_im = (ab_im * lr - (ab_re - 1.0) * li) / den
    br = b_re.astype(F32)
    bi = b_im.astype(F32)
    bb_re = z_re[..., None] * br - z_im[..., None] * bi
    bb_im = z_re[..., None] * bi + z_im[..., None] * br
    eye = jnp.eye(gpb, dtype=F32)

    def in_block(bb):
        t = bb.reshape(nblk, gpb, states, C_GROUP)
        return jnp.einsum('ngph,gk->nghkp', t, eye).reshape(nblk, gpb * C_GROUP, gpb * states)

    bmat = jnp.concatenate([in_block(bb_re), in_block(bb_im)], axis=-1).astype(BF16)

    def out_block(cc):
        t = cc.astype(F32).reshape(nblk, gpb, C_GROUP, states)
        return jnp.einsum('nghp,gk->ngpkh', t, eye).reshape(nblk, gpb * states, gpb * C_GROUP)

    cmat = jnp.concatenate([out_block(c_re), -out_block(c_im)], axis=1).astype(BF16)

    def powers(n):
        n = n.astype(F32)[:, None, None]
        m = jnp.exp(n * (lr * dt)[None])
        ang = n * (li * dt)[None]
        pr = (m * jnp.cos(ang)).reshape(-1, nblk, gpb * states).transpose(1, 0, 2)
        pi = (m * jnp.sin(ang)).reshape(-1, nblk, gpb * states).transpose(1, 0, 2)
        return jnp.stack([pr, pi], axis=1)

    nj = ts // SUBLANES
    acoef = jnp.broadcast_to(powers(jnp.ones((1,)))[:, :, 0:1, :], (nblk, 2, SUBLANES, gpb * states))
    apow = powers(jnp.arange(1, nj + 1))
    atile = powers(jnp.full((1,), nj))
    return bmat, cmat, acoef, apow, atile


def _s5_layer(h, gain, w_in, lam_re, lam_im, log_dt, b_re, b_im, c_re, c_im, d_skip, w_gate, b_gate,
              w_out, batch):
    rows, d = h.shape
    seq = rows // batch
    ts = S5_TILE
    nj = ts // SUBLANES
    per_batch = seq // ts
    bmat, cmat, acoef, apow, atile = _s5_discretise(lam_re, lam_im, log_dt, b_re, b_im, c_re, c_im, ts)
    nblk, _, sw = bmat.shape
    p = jnp.arange(ts)
    perm = (p[None, :] == ((p % SUBLANES) * nj + p // SUBLANES)[:, None]).astype(BF16)
    tile = lambda b, i: (b * per_batch + i, 0)
    return pl.pallas_call(
        _s5_kernel,
        grid=(batch, per_batch),
        in_specs=[pl.BlockSpec((ts, d), tile), _whole((1, d)), _whole((d, d)), _whole((ts, ts)), _whole((ts, ts)),
                  _whole(bmat.shape), _whole(acoef.shape), _whole(apow.shape), _whole(atile.shape),
                  _whole(cmat.shape), _whole((1, d)), _whole((d, d)), _whole((1, d)), _whole((d, d))],
        out_specs=pl.BlockSpec((ts, d), tile),
        out_shape=jax.ShapeDtypeStruct((rows, d), F32),
        scratch_shapes=[pltpu.VMEM((nblk, SUBLANES, sw), F32), pltpu.VMEM((ts, sw), F32),
                        pltpu.VMEM((ts, sw), F32), pltpu.VMEM((ts, sw), BF16),
                        pltpu.VMEM((ts, d), F32)],
        compiler_params=_params(("parallel", "arbitrary")),
        name="s5_mix",
    )(h, gain, w_in.astype(BF16), perm, perm.T, bmat, acoef, apow, atile, cmat, d_skip.reshape(1, d),
      w_gate.astype(BF16), b_gate.reshape(1, d), w_out.astype(BF16))


def kernel(x, mem, norm_mix, norm_x, norm_ffn, norm_mem, norm_final, a_w_in, a_conv_w, a_gate_b, a_norm, a_w_out, b_w_in, b_gate_w2, b_gate_b, b_norm, b_w_out, c_w_in, c_lam_re, c_lam_im, c_log_dt, c_b_re, c_b_im, c_c_re, c_c_im, c_d, c_w_gate, c_b_gate, c_w_out, x_w_q, x_w_kv, x_w_o, f_w_gate, f_w_up, f_w_down):
    batch, seq, d = x.shape
    mem_len = mem.shape[1]
    depth = norm_mix.shape[0]
    assert seq % ROW_TILE == 0 and seq % MLSTM_TILE == 0 and seq % GLA_TILE == 0 and seq % S5_TILE == 0
    assert MLSTM_TILE % MIX_CHUNK == 0 and GLA_TILE % MIX_CHUNK == 0
    assert f_w_gate.shape[2] % FF_CHUNK == 0 and d % (X_HEADS * LANES) == 0

    kv = _mem_kv(mem.reshape(batch * mem_len, d), norm_mem.reshape(1, d), x_w_kv.astype(BF16), mem_len)

    h = x.reshape(batch * seq, d)
    for i in range(depth):
        kind = i % N_MIXERS
        j = i // N_MIXERS
        gain = norm_mix[i].reshape(1, d)
        if kind == 0:
            h = _mlstm_layer(h, gain, a_w_in[j], a_conv_w[j], a_gate_b[j], a_norm[j], a_w_out[j], batch)
        elif kind == 1:
            h = _gla_layer(h, gain, b_w_in[j], b_gate_w2[j], b_gate_b[j], b_norm[j], b_w_out[j], batch)
        else:
            h = _s5_layer(h, gain, c_w_in[j], c_lam_re[j], c_lam_im[j], c_log_dt[j], c_b_re[j], c_b_im[j],
                          c_c_re[j], c_c_im[j], c_d[j], c_w_gate[j], c_b_gate[j], c_w_out[j], batch)
        h = _xattn(h, norm_x[i].reshape(1, d), x_w_q[i].astype(BF16), kv, i, x_w_o[i].astype(BF16),
                   batch, mem_len)
        h = _swiglu(h, norm_ffn[i].reshape(1, d), f_w_gate[i].astype(BF16), f_w_up[i].astype(BF16),
                    f_w_down[i].astype(BF16), norm_final.reshape(1, d), final_norm=(i == depth - 1))
    return h.reshape(batch, seq, d)
```

```python
import functools
import math

import jax
import jax.numpy as jnp
from jax import lax
from jax.experimental import pallas as pl
from jax.experimental.pallas import tpu as pltpu

F32 = jnp.float32
BF16 = jnp.bfloat16
EPS = 1e-6

N_MIXERS = 3
A_HEADS = 4
A_CONV = 4
B_HEADS = 4
B_RANK = 16
B_TAU = 16.0
C_GROUP = 16
C_STATE = 64
X_HEADS = 4

LANES = 128
SUBLANES = 8
VMEM_LIMIT_BYTES = 56 * 1024 * 1024

ROW_TILE = 512
SUB_TILE = 256
MLSTM_TILE = 256
GLA_TILE = 512
MIX_CHUNK = 128
S5_TILE = 256
S5_UNROLL = 32
FF_CHUNK = 256
S5_BLOCK_GROUPS = LANES // C_GROUP


def _params(semantics):
    return pltpu.CompilerParams(dimension_semantics=semantics, vmem_limit_bytes=VMEM_LIMIT_BYTES)


def _whole(shape):
    zeros = (0,) * len(shape)
    return pl.BlockSpec(shape, lambda *_: zeros)


def _rms(x, g):
    return x * lax.rsqrt(jnp.mean(x * x, axis=-1, keepdims=True) + EPS) * g


def _dot(a, b):
    return jnp.dot(a, b, preferred_element_type=F32)


def _dot_nt(a, b):
    return lax.dot_general(a, b, (((1,), (1,)), ((), ())), preferred_element_type=F32)


def _dot_tn(a, b):
    return lax.dot_general(a, b, (((0,), (0,)), ((), ())), preferred_element_type=F32)


def _split3(x):
    hi = x.astype(BF16)
    r = x - hi.astype(F32)
    mid = r.astype(BF16)
    lo = (r - mid.astype(F32)).astype(BF16)
    return hi, mid, lo


def _dot_exact_lhs01(m01, x):
    hi, mid, lo = _split3(x)
    return _dot(m01, hi) + _dot(m01, mid) + _dot(m01, lo)


def _head_rms(o, g):
    return o * lax.rsqrt(jnp.mean(o * o, axis=-1, keepdims=True) + EPS) * g


def _mem_kv_kernel(x_ref, g_ref, w_ref, o_ref):
    xn = _rms(x_ref[...], g_ref[...]).astype(BF16)
    o_ref[...] = _dot(xn, w_ref[0]).astype(BF16)


def _mem_kv(x, g, w, tm):
    rows, d = x.shape
    depth, _, n = w.shape
    return pl.pallas_call(
        _mem_kv_kernel,
        grid=(depth, rows // tm),
        in_specs=[pl.BlockSpec((tm, d), lambda l, i: (i, 0)), _whole((1, d)),
                  pl.BlockSpec((1, d, n), lambda l, i: (l, 0, 0))],
        out_specs=pl.BlockSpec((tm, n), lambda l, i: (i, l)),
        out_shape=jax.ShapeDtypeStruct((rows, depth * n), BF16),
        compiler_params=_params(("parallel", "parallel")),
        name="mem_kv_proj",
    )(x, g, w)


def _xattn_kernel(h_ref, g_ref, wq_ref, kv_ref, wo_ref, o_ref):
    d = h_ref.shape[1]
    dh = d // X_HEADS
    x = h_ref[...]
    xn = _rms(x, g_ref[...]).astype(BF16)
    q = _dot(xn, wq_ref[...]).astype(BF16)
    outs = []
    for hd in range(X_HEADS):
        qh = q[:, hd * dh:(hd + 1) * dh]
        kh = kv_ref[:, hd * dh:(hd + 1) * dh]
        vh = kv_ref[:, d + hd * dh:d + (hd + 1) * dh]
        s = _dot_nt(qh, kh) * (dh ** -0.5)
        e = jnp.exp(s - jnp.max(s, axis=-1, keepdims=True))
        p = e / jnp.sum(e, axis=-1, keepdims=True)
        outs.append(_dot(p.astype(BF16), vh))
    o = jnp.concatenate(outs, axis=-1).astype(BF16)
    o_ref[...] = x + _dot(o, wo_ref[...])


def _xattn(h, g, wq, kv, layer, wo, batch, mem_len):
    rows, d = h.shape
    tm = ROW_TILE
    per_batch = rows // batch // tm
    return pl.pallas_call(
        _xattn_kernel,
        grid=(batch, per_batch),
        in_specs=[
            pl.BlockSpec((tm, d), lambda b, i: (b * per_batch + i, 0)),
            _whole((1, d)),
            _whole((d, d)),
            pl.BlockSpec((mem_len, 2 * d), lambda b, i: (b, layer)),
            _whole((d, d)),
        ],
        out_specs=pl.BlockSpec((tm, d), lambda b, i: (b * per_batch + i, 0)),
        out_shape=jax.ShapeDtypeStruct((rows, d), F32),
        compiler_params=_params(("parallel", "parallel")),
        name="xattn",
    )(h, g, wq, kv, wo)


def _swiglu_kernel(h_ref, g_ref, wg_ref, wu_ref, wd_ref, gf_ref, o_ref, *, final_norm):
    x = h_ref[...]
    xn = _rms(x, g_ref[...]).astype(BF16)
    o_ref[...] = x
    for c in range(wg_ref.shape[1] // FF_CHUNK):
        cols = slice(c * FF_CHUNK, (c + 1) * FF_CHUNK)
        gate = _dot(xn, wg_ref[:, cols])
        up = _dot(xn, wu_ref[:, cols])
        act = (jax.nn.silu(gate) * up).astype(BF16)
        o_ref[...] += _dot(act, wd_ref[cols, :])
    if final_norm:
        o_ref[...] = _rms(o_ref[...], gf_ref[...])


def _swiglu(h, g, wg, wu, wd, gf, final_norm):
    rows, d = h.shape
    tm = ROW_TILE
    return pl.pallas_call(
        functools.partial(_swiglu_kernel, final_norm=final_norm),
        grid=(rows // tm,),
        in_specs=[pl.BlockSpec((tm, d), lambda i: (i, 0)), _whole((1, d)), _whole(wg.shape),
                  _whole(wu.shape), _whole(wd.shape), _whole((1, d))],
        out_specs=pl.BlockSpec((tm, d), lambda i: (i, 0)),
        out_shape=jax.ShapeDtypeStruct((rows, d), F32),
        compiler_params=_params(("parallel",)),
        name="swiglu",
    )(h, g, wg, wu, wd, gf)


def _mlstm_project(h_ref, g_ref, wqk_ref, wv_ref, wo_ref, wgt_ref, cw_ref, gb_ref,
                   q_ref, k_ref, v_ref, og_ref, gr_ref, xp_ref):
    tm, d = h_ref.shape
    nqk = wqk_ref.shape[1]
    dk = nqk // (2 * A_HEADS)
    cw = cw_ref[...]
    for r0 in range(0, tm, SUB_TILE):
        rows = slice(r0, r0 + SUB_TILE)
        xn = _rms(h_ref[rows, :], g_ref[...]).astype(BF16)
        qk = _dot(xn, wqk_ref[...])
        xp_ref[SUBLANES + r0:SUBLANES + r0 + SUB_TILE, :] = qk
        y = cw[A_CONV - 1:A_CONV, :] * qk
        for j in range(A_CONV - 1):
            y = y + cw[j:j + 1, :] * xp_ref[pl.ds(r0 + SUBLANES - (A_CONV - 1) + j, SUB_TILE), :]
        y = jax.nn.silu(y)
        q_ref[rows, :] = (y[:, :nqk // 2] * (dk ** -0.5)).astype(BF16)
        k_ref[rows, :] = y[:, nqk // 2:].astype(BF16)
        v_ref[rows, :] = _dot(xn, wv_ref[...]).astype(BF16)
        og_ref[rows, :] = jax.nn.sigmoid(_dot(xn, wo_ref[...])).astype(BF16)
        gr = _dot_nt(wgt_ref[...], xn) + gb_ref[...]
        is_input_gate = lax.broadcasted_iota(jnp.int32, gr.shape, 0) < A_HEADS
        gr_ref[:, rows] = jnp.where(is_input_gate, gr, jax.nn.log_sigmoid(gr))
    xp_ref[0:SUBLANES, :] = xp_ref[tm:tm + SUBLANES, :]


def _mlstm_recur(q_ref, k_ref, v_ref, og_ref, gr_ref, ng_ref, wout_ref, h_ref, o_ref,
                 c_ref, n_ref, m_ref, y_ref):
    ts = q_ref.shape[0]
    dk = q_ref.shape[1] // A_HEADS
    dv = v_ref.shape[1] // A_HEADS
    L = MIX_CHUNK
    row = lax.broadcasted_iota(jnp.int32, (L, L), 0)
    col = lax.broadcasted_iota(jnp.int32, (L, L), 1)
    causal = col <= row
    eye = col == row

    def to_col(x_row):
        return jnp.sum(jnp.where(eye, x_row, 0.0), axis=1, keepdims=True)

    for c in range(ts // L):
        rows = slice(c * L, (c + 1) * L)
        for hd in range(A_HEADS):
            i_row = gr_ref[hd:hd + 1, rows]
            f_row = gr_ref[A_HEADS + hd:A_HEADS + hd + 1, rows]
            b_col = jnp.sum(jnp.where(causal, f_row, 0.0), axis=1, keepdims=True)
            b_row = jnp.sum(jnp.where(eye, b_col, 0.0), axis=0, keepdims=True)
            i_col = to_col(i_row)
            m_prev = m_ref[hd][0:1, 0:1]
            dmat = jnp.where(causal, b_col - b_row + i_row, -jnp.inf)
            inter = b_col + m_prev
            m_t = jnp.maximum(inter, jnp.max(dmat, axis=1, keepdims=True))
            w_intra = jnp.exp(dmat - m_t)
            w_inter = jnp.exp(inter - m_t)
            qh = q_ref[rows, hd * dk:(hd + 1) * dk]
            kh = k_ref[rows, hd * dk:(hd + 1) * dk]
            vh = v_ref[rows, hd * dv:(hd + 1) * dv]
            c_prev = c_ref[hd]
            n_prev = n_ref[hd][0:1, :]
            s = _dot_nt(qh, kh) * w_intra
            num = _dot(s.astype(BF16), vh) + w_inter * _dot(qh, c_prev.astype(BF16))
            qn = jnp.sum(qh.astype(F32) * n_prev, axis=1, keepdims=True)
            den = jnp.sum(s, axis=1, keepdims=True) + w_inter * qn
            hh = num / jnp.maximum(jnp.abs(den), jnp.exp(-m_t))
            g_tot = jnp.sum(f_row, axis=1, keepdims=True)
            a_col = g_tot - b_col + i_col
            m_new = jnp.maximum(g_tot + m_prev, jnp.max(a_col, axis=0, keepdims=True))
            decay = jnp.exp(g_tot + m_prev - m_new)
            kw = kh.astype(F32) * jnp.exp(a_col - m_new)
            c_ref[hd] = decay * c_prev + _dot_tn(kw.astype(BF16), vh)
            n_new = decay * n_prev + jnp.sum(kw, axis=0, keepdims=True)
            n_ref[hd] = jnp.broadcast_to(n_new, n_ref.shape[1:])
            m_ref[hd] = jnp.broadcast_to(m_new, m_ref.shape[1:])
            cols = slice(hd * dv, (hd + 1) * dv)
            yh = _head_rms(hh, ng_ref[:, cols]) * og_ref[rows, cols].astype(F32)
            y_ref[rows, cols] = yh.astype(BF16)

    o_ref[...] = h_ref[...] + _dot(y_ref[...], wout_ref[...])


def _mlstm_in_kernel(h_ref, g_ref, wqk_ref, wv_ref, wo_ref, wgt_ref, cw_ref, gb_ref,
                     q_ref, k_ref, v_ref, og_ref, gr_ref, xp_ref, *, tiles_per_seq):
    @pl.when(pl.program_id(0) % tiles_per_seq == 0)
    def _():
        xp_ref[0:SUBLANES, :] = jnp.zeros((SUBLANES, xp_ref.shape[1]), F32)

    _mlstm_project(h_ref, g_ref, wqk_ref, wv_ref, wo_ref, wgt_ref, cw_ref, gb_ref,
                   q_ref, k_ref, v_ref, og_ref, gr_ref, xp_ref)


def _mlstm_scan_kernel(q_ref, k_ref, v_ref, og_ref, gr_ref, ng_ref, wout_ref, h_ref, o_ref,
                       c_ref, n_ref, m_ref, y_ref):
    @pl.when(pl.program_id(1) == 0)
    def _():
        c_ref[...] = jnp.zeros(c_ref.shape, F32)
        n_ref[...] = jnp.zeros(n_ref.shape, F32)
        m_ref[...] = jnp.zeros(m_ref.shape, F32)

    _mlstm_recur(q_ref, k_ref, v_ref, og_ref, gr_ref, ng_ref, wout_ref, h_ref, o_ref,
                 c_ref, n_ref, m_ref, y_ref)


def _mlstm_layer(h, gain, w_in, conv_w, gate_b, norm_g, w_out, batch):
    rows, d = h.shape
    seq = rows // batch
    nqk = conv_w.shape[1]
    dk = nqk // (2 * A_HEADS)
    dv = d // A_HEADS
    nv = A_HEADS * dv
    ng = 2 * A_HEADS
    wqk = w_in[:, :nqk].astype(BF16)
    wv = w_in[:, nqk:nqk + nv].astype(BF16)
    wo = w_in[:, nqk + nv:nqk + 2 * nv].astype(BF16)
    wgt = w_in[:, nqk + 2 * nv:].T.astype(BF16)
    tm = ROW_TILE
    q, k, v, og, gr = pl.pallas_call(
        functools.partial(_mlstm_in_kernel, tiles_per_seq=seq // tm),
        grid=(rows // tm,),
        in_specs=[pl.BlockSpec((tm, d), lambda i: (i, 0)), _whole((1, d)), _whole((d, nqk)),
                  _whole((d, nv)), _whole((d, nv)), _whole((ng, d)), _whole((A_CONV, nqk)),
                  _whole((ng, 1))],
        out_specs=[pl.BlockSpec((tm, nqk // 2), lambda i: (i, 0)),
                   pl.BlockSpec((tm, nqk // 2), lambda i: (i, 0)),
                   pl.BlockSpec((tm, nv), lambda i: (i, 0)),
                   pl.BlockSpec((tm, nv), lambda i: (i, 0)),
                   pl.BlockSpec((ng, tm), lambda i: (0, i))],
        out_shape=[jax.ShapeDtypeStruct((rows, nqk // 2), BF16),
                   jax.ShapeDtypeStruct((rows, nqk // 2), BF16),
                   jax.ShapeDtypeStruct((rows, nv), BF16),
                   jax.ShapeDtypeStruct((rows, nv), BF16),
                   jax.ShapeDtypeStruct((ng, rows), F32)],
        scratch_shapes=[pltpu.VMEM((tm + SUBLANES, nqk), F32)],
        compiler_params=_params(("arbitrary",)),
        name="mlstm_in",
    )(h, gain, wqk, wv, wo, wgt, conv_w, gate_b.reshape(ng, 1))

    ts = MLSTM_TILE
    per_batch = seq // ts
    tile = lambda b, i: (b * per_batch + i, 0)
    return pl.pallas_call(
        _mlstm_scan_kernel,
        grid=(batch, per_batch),
        in_specs=[pl.BlockSpec((ts, nqk // 2), tile), pl.BlockSpec((ts, nqk // 2), tile),
                  pl.BlockSpec((ts, nv), tile), pl.BlockSpec((ts, nv), tile),
                  pl.BlockSpec((ng, ts), lambda b, i: (0, b * per_batch + i)),
                  _whole((1, nv)), _whole((nv, d)), pl.BlockSpec((ts, d), tile)],
        out_specs=pl.BlockSpec((ts, d), tile),
        out_shape=jax.ShapeDtypeStruct((rows, d), F32),
        scratch_shapes=[pltpu.VMEM((A_HEADS, dk, dv), F32),
                        pltpu.VMEM((A_HEADS, SUBLANES, dk), F32),
                        pltpu.VMEM((A_HEADS, SUBLANES, LANES), F32),
                        pltpu.VMEM((ts, nv), BF16)],
        compiler_params=_params(("parallel", "arbitrary")),
        name="mlstm_scan",
    )(q, k, v, og, gr, norm_g.reshape(1, nv), w_out.astype(BF16), h)


def _gla_in_kernel(h_ref, g_ref, wq_ref, wk_ref, wv_ref, wr_ref, wg1_ref, wg2_ref, gb_ref,
                   q_ref, k_ref, v_ref, rg_ref, la_ref):
    dk = wq_ref.shape[1] // B_HEADS
    xn = _rms(h_ref[...], g_ref[...]).astype(BF16)
    q_ref[...] = (_dot(xn, wq_ref[...]) * (dk ** -0.5)).astype(BF16)
    k_ref[...] = _dot(xn, wk_ref[...]).astype(BF16)
    v_ref[...] = _dot(xn, wv_ref[...]).astype(BF16)
    rg_ref[...] = jax.nn.silu(_dot(xn, wr_ref[...])).astype(BF16)
    g_low = _dot(xn, wg1_ref[...]).astype(BF16)
    la_ref[...] = jax.nn.log_sigmoid(_dot(g_low, wg2_ref[...]) + gb_ref[...]) / B_TAU


def _bcast_sublane(x3, idx):
    return jnp.broadcast_to(x3[:, idx:idx + 1, :], x3.shape)


def _boundary_rows(b, half):
    L, w = b.shape
    if half >= SUBLANES:
        nb = L // (2 * half)
        b4 = b.reshape(nb, 2 * half // SUBLANES, SUBLANES, w)
        last = b4[:, half // SUBLANES - 1, SUBLANES - 1:SUBLANES, :]
        return jnp.broadcast_to(last[:, None, :, :], b4.shape).reshape(L, w)
    b3 = b.reshape(L // SUBLANES, SUBLANES, w)
    sub = lax.broadcasted_iota(jnp.int32, b3.shape, 1)
    out = None
    for blk in range(SUBLANES // (2 * half)):
        cand = _bcast_sublane(b3, blk * 2 * half + half - 1)
        out = cand if out is None else jnp.where(sub >= blk * 2 * half, cand, out)
    return out.reshape(L, w)


def _gla_scan_kernel(q_ref, k_ref, v_ref, rg_ref, la_ref, ng_ref, wout_ref, h_ref, o_ref,
                     st_ref, y_ref):
    ts = q_ref.shape[0]
    dk = q_ref.shape[1] // B_HEADS
    dv = v_ref.shape[1] // B_HEADS
    L = MIX_CHUNK

    @pl.when(pl.program_id(1) == 0)
    def _():
        st_ref[...] = jnp.zeros(st_ref.shape, F32)

    row = lax.broadcasted_iota(jnp.int32, (L, L), 0)
    col = lax.broadcasted_iota(jnp.int32, (L, L), 1)
    tri = (col <= row).astype(BF16)
    eye = col == row
    eye_k = (lax.broadcasted_iota(jnp.int32, (dk, dk), 0) == lax.broadcasted_iota(jnp.int32, (dk, dk), 1))
    pos = lax.broadcasted_iota(jnp.int32, (L, dk), 0)

    halves = []
    half = L // 2
    while half >= 1:
        halves.append(half)
        half //= 2

    for c in range(ts // L):
        rows = slice(c * L, (c + 1) * L)
        b_all = _dot_exact_lhs01(tri, la_ref[rows, :])
        for hd in range(B_HEADS):
            kc = slice(hd * dk, (hd + 1) * dk)
            vc = slice(hd * dv, (hd + 1) * dv)
            b = b_all[:, kc]
            qh = q_ref[rows, kc].astype(F32)
            kh = k_ref[rows, kc].astype(F32)
            vh = v_ref[rows, vc]
            s_mat = jnp.where(eye, _dot_nt(qh.astype(BF16), kh.astype(BF16)), 0.0)
            for half in halves:
                ref_b = _boundary_rows(b, half)
                late = (pos & half) != 0
                qt = qh * jnp.exp(jnp.where(late, b - ref_b, -jnp.inf))
                kt = kh * jnp.exp(jnp.where(late, -jnp.inf, ref_b - b))
                same_block = (row // (2 * half)) == (col // (2 * half))
                s_mat = s_mat + jnp.where(same_block, _dot_nt(qt.astype(BF16), kt.astype(BF16)), 0.0)
            state = st_ref[hd]
            o = _dot(s_mat.astype(BF16), vh) + _dot((qh * jnp.exp(b)).astype(BF16), state.astype(BF16))
            b_last = b[L - 1:L, :]
            k_dec = kh * jnp.exp(b_last - b)
            decay_col = jnp.sum(jnp.where(eye_k, jnp.exp(b_last), 0.0), axis=1, keepdims=True)
            st_ref[hd] = decay_col * state + _dot_tn(k_dec.astype(BF16), vh)
            yh = _head_rms(o, ng_ref[:, vc]) * rg_ref[rows, vc].astype(F32)
            y_ref[rows, vc] = yh.astype(BF16)

    o_ref[...] = h_ref[...] + _dot(y_ref[...], wout_ref[...])


def _gla_layer(h, gain, w_in, gate_w2, gate_b, norm_g, w_out, batch):
    rows, d = h.shape
    seq = rows // batch
    nk = gate_w2.shape[1]
    dk = nk // B_HEADS
    dv = d // B_HEADS
    nv = B_HEADS * dv
    wq = w_in[:, :nk].astype(BF16)
    wk = w_in[:, nk:2 * nk].astype(BF16)
    wv = w_in[:, 2 * nk:2 * nk + nv].astype(BF16)
    wr = w_in[:, 2 * nk + nv:2 * nk + 2 * nv].astype(BF16)
    wg1 = jnp.pad(w_in[:, 2 * nk + 2 * nv:], ((0, 0), (0, LANES - B_RANK))).astype(BF16)
    wg2 = jnp.pad(gate_w2, ((0, LANES - B_RANK), (0, 0))).astype(BF16)
    tm = ROW_TILE
    row_tile = lambda i: (i, 0)
    q, k, v, rg, la = pl.pallas_call(
        _gla_in_kernel,
        grid=(rows // tm,),
        in_specs=[pl.BlockSpec((tm, d), row_tile), _whole((1, d)), _whole((d, nk)), _whole((d, nk)),
                  _whole((d, nv)), _whole((d, nv)), _whole((d, LANES)), _whole((LANES, nk)),
                  _whole((1, nk))],
        out_specs=[pl.BlockSpec((tm, nk), row_tile), pl.BlockSpec((tm, nk), row_tile),
                   pl.BlockSpec((tm, nv), row_tile), pl.BlockSpec((tm, nv), row_tile),
                   pl.BlockSpec((tm, nk), row_tile)],
        out_shape=[jax.ShapeDtypeStruct((rows, nk), BF16), jax.ShapeDtypeStruct((rows, nk), BF16),
                   jax.ShapeDtypeStruct((rows, nv), BF16), jax.ShapeDtypeStruct((rows, nv), BF16),
                   jax.ShapeDtypeStruct((rows, nk), F32)],
        compiler_params=_params(("parallel",)),
        name="gla_in",
    )(h, gain, wq, wk, wv, wr, wg1, wg2, gate_b.reshape(1, nk))

    ts = GLA_TILE
    per_batch = seq // ts
    tile = lambda b, i: (b * per_batch + i, 0)
    return pl.pallas_call(
        _gla_scan_kernel,
        grid=(batch, per_batch),
        in_specs=[pl.BlockSpec((ts, nk), tile), pl.BlockSpec((ts, nk), tile),
                  pl.BlockSpec((ts, nv), tile), pl.BlockSpec((ts, nv), tile),
                  pl.BlockSpec((ts, nk), tile), _whole((1, nv)), _whole((nv, d)),
                  pl.BlockSpec((ts, d), tile)],
        out_specs=pl.BlockSpec((ts, d), tile),
        out_shape=jax.ShapeDtypeStruct((rows, d), F32),
        scratch_shapes=[pltpu.VMEM((B_HEADS, dk, dv), F32), pltpu.VMEM((ts, nv), BF16)],
        compiler_params=_params(("parallel", "arbitrary")),
        name="gla_scan",
    )(q, k, v, rg, la, norm_g.reshape(1, nv), w_out.astype(BF16), h)


def _s5_kernel(h_ref, g_ref, win_ref, perm_ref, permt_ref, bmat_ref, acoef_ref, apow_ref, atile_ref,
               cmat_ref, dskip_ref, wg_ref, bg_ref, wo_ref, o_ref, carry_ref, bu_ref, s_ref, sb_ref, yp_ref):
    ts, d = h_ref.shape
    nblk = bmat_ref.shape[0]
    half = bmat_ref.shape[2] // 2
    nj = ts // SUBLANES

    @pl.when(pl.program_id(1) == 0)
    def _():
        carry_ref[...] = jnp.zeros(carry_ref.shape, F32)

    xn = _rms(h_ref[...], g_ref[...]).astype(BF16)
    xp = _dot(perm_ref[...], xn).astype(BF16)
    u = _dot(xp, win_ref[...])
    ub = u.astype(BF16)

    for blk in range(nblk):
        bu_ref[...] = _dot(ub[:, blk * LANES:(blk + 1) * LANES], bmat_ref[blk])
        a_re = acoef_ref[blk, 0]
        a_im = acoef_ref[blk, 1]

        def local_step(j, x):
            x_re, x_im = x
            r = pl.ds(pl.multiple_of(j * SUBLANES, SUBLANES), SUBLANES)
            n_re = a_re * x_re - a_im * x_im + bu_ref[r, 0:half]
            n_im = a_re * x_im + a_im * x_re + bu_ref[r, half:2 * half]
            s_ref[r, 0:half] = n_re
            s_ref[r, half:2 * half] = n_im
            return n_re, n_im

        zero = jnp.zeros((SUBLANES, half), F32)
        end_re, end_im = lax.fori_loop(0, nj, local_step, (zero, zero), unroll=S5_UNROLL)

        t_re = atile_ref[blk, 0]
        t_im = atile_ref[blk, 1]
        c_re = carry_ref[blk, 0:1, 0:half]
        c_im = carry_ref[blk, 0:1, half:2 * half]
        cin_re, cin_im = [], []
        for r in range(SUBLANES):
            cin_re.append(c_re)
            cin_im.append(c_im)
            e_re = end_re[r:r + 1, :] + t_re * c_re - t_im * c_im
            e_im = end_im[r:r + 1, :] + t_re * c_im + t_im * c_re
            c_re, c_im = e_re, e_im
        carry_ref[blk, :, 0:half] = jnp.broadcast_to(c_re, (SUBLANES, half))
        carry_ref[blk, :, half:2 * half] = jnp.broadcast_to(c_im, (SUBLANES, half))
        cin_re = jnp.concatenate(cin_re, axis=0)
        cin_im = jnp.concatenate(cin_im, axis=0)

        def fix_step(jj, _):
            out_re, out_im = [], []
            for k in range(2):
                j = 2 * jj + k
                r = pl.ds(pl.multiple_of(j * SUBLANES, SUBLANES), SUBLANES)
                p_re = apow_ref[blk, 0, pl.ds(j, 1), :]
                p_im = apow_ref[blk, 1, pl.ds(j, 1), :]
                out_re.append(s_ref[r, 0:half] + p_re * cin_re - p_im * cin_im)
                out_im.append(s_ref[r, half:2 * half] + p_re * cin_im + p_im * cin_re)
            r2 = pl.ds(pl.multiple_of(jj * 2 * SUBLANES, 2 * SUBLANES), 2 * SUBLANES)
            sb_ref[r2, 0:half] = jnp.concatenate(out_re, axis=0).astype(BF16)
            sb_ref[r2, half:2 * half] = jnp.concatenate(out_im, axis=0).astype(BF16)
            return 0

        lax.fori_loop(0, nj // 2, fix_step, 0, unroll=S5_UNROLL // 2)
        yp_ref[:, blk * LANES:(blk + 1) * LANES] = _dot(sb_ref[...], cmat_ref[blk])

    y = jax.nn.gelu(yp_ref[...] + dskip_ref[...] * u)
    z = y * jax.nn.sigmoid(_dot(y.astype(BF16), wg_ref[...]) + bg_ref[...])
    z = _dot(permt_ref[...], z.astype(BF16)).astype(BF16)
    o_ref[...] = h_ref[...] + _dot(z, wo_ref[...])


def _s5_discretise(lam_re, lam_im, log_dt, b_re, b_im, c_re, c_im, ts):
    groups, states = lam_re.shape
    gpb = S5_BLOCK_GROUPS
    nblk = groups // gpb
    lr = lam_re.astype(F32)
    li = lam_im.astype(F32)
    dt = jnp.exp(log_dt.astype(F32))[:, None]
    mag = jnp.exp(lr * dt)
    ab_re = mag * jnp.cos(li * dt)
    ab_im = mag * jnp.sin(li * dt)
    den = lr * lr + li * li
    z_re = ((ab_re - 1.0) * lr + ab_im * li) / den
    z_im = (ab_im * lr - (ab_re - 1.0) * li) / den
    br = b_re.astype(F32)
    bi = b_im.astype(F32)
    bb_re = z_re[..., None] * br - z_im[..., None] * bi
    bb_im = z_re[..., None] * bi + z_im[..., None] * br
    eye = jnp.eye(gpb, dtype=F32)

    def in_block(bb):
        t = bb.reshape(nblk, gpb, states, C_GROUP)
        return jnp.einsum('ngph,gk->nghkp', t, eye).reshape(nblk, gpb * C_GROUP, gpb * states)

    bmat = jnp.concatenate([in_block(bb_re), in_block(bb_im)], axis=-1).astype(BF16)

    def out_block(cc):
        t = cc.astype(F32).reshape(nblk, gpb, C_GROUP, states)
        return jnp.einsum('nghp,gk->ngpkh', t, eye).reshape(nblk, gpb * states, gpb * C_GROUP)

    cmat = jnp.concatenate([out_block(c_re), -out_block(c_im)], axis=1).astype(BF16)

    def powers(n):
        n = n.astype(F32)[:, None, None]
        m = jnp.exp(n * (lr * dt)[None])
        ang = n * (li * dt)[None]
        pr = (m * jnp.cos(ang)).reshape(-1, nblk, gpb * states).transpose(1, 0, 2)
        pi = (m * jnp.sin(ang)).reshape(-1, nblk, gpb * states).transpose(1, 0, 2)
        return jnp.stack([pr, pi], axis=1)

    nj = ts // SUBLANES
    acoef = jnp.broadcast_to(powers(jnp.ones((1,)))[:, :, 0:1, :], (nblk, 2, SUBLANES, gpb * states))
    apow = powers(jnp.arange(1, nj + 1))
    atile = powers(jnp.full((1,), nj))
    return bmat, cmat, acoef, apow, atile


def _s5_layer(h, gain, w_in, lam_re, lam_im, log_dt, b_re, b_im, c_re, c_im, d_skip, w_gate, b_gate,
              w_out, batch):
    rows, d = h.shape
    seq = rows // batch
    ts = S5_TILE
    nj = ts // SUBLANES
    per_batch = seq // ts
    bmat, cmat, acoef, apow, atile = _s5_discretise(lam_re, lam_im, log_dt, b_re, b_im, c_re, c_im, ts)
    nblk, _, sw = bmat.shape
    p = jnp.arange(ts)
    perm = (p[None, :] == ((p % SUBLANES) * nj + p // SUBLANES)[:, None]).astype(BF16)
    tile = lambda b, i: (b * per_batch + i, 0)
    return pl.pallas_call(
        _s5_kernel,
        grid=(batch, per_batch),
        in_specs=[pl.BlockSpec((ts, d), tile), _whole((1, d)), _whole((d, d)), _whole((ts, ts)), _whole((ts, ts)),
                  _whole(bmat.shape), _whole(acoef.shape), _whole(apow.shape), _whole(atile.shape),
                  _whole(cmat.shape), _whole((1, d)), _whole((d, d)), _whole((1, d)), _whole((d, d))],
        out_specs=pl.BlockSpec((ts, d), tile),
        out_shape=jax.ShapeDtypeStruct((rows, d), F32),
        scratch_shapes=[pltpu.VMEM((nblk, SUBLANES, sw), F32), pltpu.VMEM((ts, sw), F32),
                        pltpu.VMEM((ts, sw), F32), pltpu.VMEM((ts, sw), BF16),
                        pltpu.VMEM((ts, d), F32)],
        compiler_params=_params(("parallel", "arbitrary")),
        name="s5_mix",
    )(h, gain, w_in.astype(BF16), perm, perm.T, bmat, acoef, apow, atile, cmat, d_skip.reshape(1, d),
      w_gate.astype(BF16), b_gate.reshape(1, d), w_out.astype(BF16))


def kernel(x, mem, norm_mix, norm_x, norm_ffn, norm_mem, norm_final, a_w_in, a_conv_w, a_gate_b, a_norm, a_w_out, b_w_in, b_gate_w2, b_gate_b, b_norm, b_w_out, c_w_in, c_lam_re, c_lam_im, c_log_dt, c_b_re, c_b_im, c_c_re, c_c_im, c_d, c_w_gate, c_b_gate, c_w_out, x_w_q, x_w_kv, x_w_o, f_w_gate, f_w_up, f_w_down):
    batch, seq, d = x.shape
    mem_len = mem.shape[1]
    depth = norm_mix.shape[0]
    assert seq % ROW_TILE == 0 and seq % MLSTM_TILE == 0 and seq % GLA_TILE == 0 and seq % S5_TILE == 0
    assert MLSTM_TILE % MIX_CHUNK == 0 and GLA_TILE % MIX_CHUNK == 0
    assert f_w_gate.shape[2] % FF_CHUNK == 0 and d % (X_HEADS * LANES) == 0

    kv = _mem_kv(mem.reshape(batch * mem_len, d), norm_mem.reshape(1, d), x_w_kv.astype(BF16), mem_len)

    h = x.reshape(batch * seq, d)
    for i in range(depth):
        kind = i % N_MIXERS
        j = i // N_MIXERS
        gain = norm_mix[i].reshape(1, d)
        if kind == 0:
            h = _mlstm_layer(h, gain, a_w_in[j], a_conv_w[j], a_gate_b[j], a_norm[j], a_w_out[j], batch)
        elif kind == 1:
            h = _gla_layer(h, gain, b_w_in[j], b_gate_w2[j], b_gate_b[j], b_norm[j], b_w_out[j], batch)
        else:
            h = _s5_layer(h, gain, c_w_in[j], c_lam_re[j], c_lam_im[j], c_log_dt[j], c_b_re[j], c_b_im[j],
                          c_c_re[j], c_c_im[j], c_d[j], c_w_gate[j], c_b_gate[j], c_w_out[j], batch)
        h = _xattn(h, norm_x[i].reshape(1, d), x_w_q[i].astype(BF16), kv, i, x_w_o[i].astype(BF16),
                   batch, mem_len)
        h = _swiglu(h, norm_ffn[i].reshape(1, d), f_w_gate[i].astype(BF16), f_w_up[i].astype(BF16),
                    f_w_down[i].astype(BF16), norm_final.reshape(1, d), final_norm=(i == depth - 1))
    return h.reshape(batch, seq, d)
```

```python
import functools
import math

import jax
import jax.numpy as jnp
from jax import lax
from jax.experimental import pallas as pl
from jax.experimental.pallas import tpu as pltpu

F32 = jnp.float32
BF16 = jnp.bfloat16
EPS = 1e-6

N_MIXERS = 3
A_HEADS = 4
A_CONV = 4
B_HEADS = 4
B_RANK = 16
B_TAU = 16.0
C_GROUP = 16
C_STATE = 64
X_HEADS = 4

LANES = 128
SUBLANES = 8
VMEM_LIMIT_BYTES = 56 * 1024 * 1024

ROW_TILE = 512
SUB_TILE = 256
MLSTM_TILE = 512
GLA_TILE = 512
MIX_CHUNK = 128
S5_TILE = 256
S5_UNROLL = 32
FF_CHUNK = 256
S5_BLOCK_GROUPS = LANES // C_GROUP


def _params(semantics):
    return pltpu.CompilerParams(dimension_semantics=semantics, vmem_limit_bytes=VMEM_LIMIT_BYTES)


def _whole(shape):
    zeros = (0,) * len(shape)
    return pl.BlockSpec(shape, lambda *_: zeros)


def _rms(x, g):
    return x * lax.rsqrt(jnp.mean(x * x, axis=-1, keepdims=True) + EPS) * g


def _dot(a, b):
    return jnp.dot(a, b, preferred_element_type=F32)


def _dot_nt(a, b):
    return lax.dot_general(a, b, (((1,), (1,)), ((), ())), preferred_element_type=F32)


def _dot_tn(a, b):
    return lax.dot_general(a, b, (((0,), (0,)), ((), ())), preferred_element_type=F32)


def _split3(x):
    hi = x.astype(BF16)
    r = x - hi.astype(F32)
    mid = r.astype(BF16)
    lo = (r - mid.astype(F32)).astype(BF16)
    return hi, mid, lo


def _dot_exact_lhs01(m01, x):
    hi, mid, lo = _split3(x)
    return _dot(m01, hi) + _dot(m01, mid) + _dot(m01, lo)


def _head_rms(o, g):
    return o * lax.rsqrt(jnp.mean(o * o, axis=-1, keepdims=True) + EPS) * g


def _mem_kv_kernel(x_ref, g_ref, w_ref, o_ref):
    xn = _rms(x_ref[...], g_ref[...]).astype(BF16)
    o_ref[...] = _dot(xn, w_ref[0]).astype(BF16)


def _mem_kv(x, g, w, tm):
    rows, d = x.shape
    depth, _, n = w.shape
    return pl.pallas_call(
        _mem_kv_kernel,
        grid=(depth, rows // tm),
        in_specs=[pl.BlockSpec((tm, d), lambda l, i: (i, 0)), _whole((1, d)),
                  pl.BlockSpec((1, d, n), lambda l, i: (l, 0, 0))],
        out_specs=pl.BlockSpec((tm, n), lambda l, i: (i, l)),
        out_shape=jax.ShapeDtypeStruct((rows, depth * n), BF16),
        compiler_params=_params(("parallel", "parallel")),
        name="mem_kv_proj",
    )(x, g, w)


def _xattn_kernel(h_ref, g_ref, wq_ref, kv_ref, wo_ref, o_ref):
    d = h_ref.shape[1]
    dh = d // X_HEADS
    x = h_ref[...]
    xn = _rms(x, g_ref[...]).astype(BF16)
    q = _dot(xn, wq_ref[...]).astype(BF16)
    outs = []
    for hd in range(X_HEADS):
        qh = q[:, hd * dh:(hd + 1) * dh]
        kh = kv_ref[:, hd * dh:(hd + 1) * dh]
        vh = kv_ref[:, d + hd * dh:d + (hd + 1) * dh]
        s = _dot_nt(qh, kh) * (dh ** -0.5)
        e = jnp.exp(s - jnp.max(s, axis=-1, keepdims=True))
        p = e / jnp.sum(e, axis=-1, keepdims=True)
        outs.append(_dot(p.astype(BF16), vh))
    o = jnp.concatenate(outs, axis=-1).astype(BF16)
    o_ref[...] = x + _dot(o, wo_ref[...])


def _xattn(h, g, wq, kv, layer, wo, batch, mem_len):
    rows, d = h.shape
    tm = ROW_TILE
    per_batch = rows // batch // tm
    return pl.pallas_call(
        _xattn_kernel,
        grid=(batch, per_batch),
        in_specs=[
            pl.BlockSpec((tm, d), lambda b, i: (b * per_batch + i, 0)),
            _whole((1, d)),
            _whole((d, d)),
            pl.BlockSpec((mem_len, 2 * d), lambda b, i: (b, layer)),
            _whole((d, d)),
        ],
        out_specs=pl.BlockSpec((tm, d), lambda b, i: (b * per_batch + i, 0)),
        out_shape=jax.ShapeDtypeStruct((rows, d), F32),
        compiler_params=_params(("parallel", "parallel")),
        name="xattn",
    )(h, g, wq, kv, wo)


def _swiglu_kernel(h_ref, g_ref, wg_ref, wu_ref, wd_ref, gf_ref, o_ref, *, final_norm):
    x = h_ref[...]
    xn = _rms(x, g_ref[...]).astype(BF16)
    o_ref[...] = x
    for c in range(wg_ref.shape[1] // FF_CHUNK):
        cols = slice(c * FF_CHUNK, (c + 1) * FF_CHUNK)
        gate = _dot(xn, wg_ref[:, cols])
        up = _dot(xn, wu_ref[:, cols])
        act = (jax.nn.silu(gate) * up).astype(BF16)
        o_ref[...] += _dot(act, wd_ref[cols, :])
    if final_norm:
        o_ref[...] = _rms(o_ref[...], gf_ref[...])


def _swiglu(h, g, wg, wu, wd, gf, final_norm):
    rows, d = h.shape
    tm = ROW_TILE
    return pl.pallas_call(
        functools.partial(_swiglu_kernel, final_norm=final_norm),
        grid=(rows // tm,),
        in_specs=[pl.BlockSpec((tm, d), lambda i: (i, 0)), _whole((1, d)), _whole(wg.shape),
                  _whole(wu.shape), _whole(wd.shape), _whole((1, d))],
        out_specs=pl.BlockSpec((tm, d), lambda i: (i, 0)),
        out_shape=jax.ShapeDtypeStruct((rows, d), F32),
        compiler_params=_params(("parallel",)),
        name="swiglu",
    )(h, g, wg, wu, wd, gf)


def _mlstm_project(h_ref, g_ref, wqk_ref, wv_ref, wo_ref, wgt_ref, cw_ref, gb_ref,
                   q_ref, k_ref, v_ref, og_ref, gr_ref, xp_ref):
    tm, d = h_ref.shape
    nqk = wqk_ref.shape[1]
    dk = nqk // (2 * A_HEADS)
    cw = cw_ref[...]
    xn_all = _rms(h_ref[...], g_ref[...]).astype(BF16)

    gr = _dot_nt(wgt_ref[...], xn_all) + gb_ref[...]
    is_input_gate = lax.broadcasted_iota(jnp.int32, gr.shape, 0) < A_HEADS
    gr = jnp.where(is_input_gate, gr, jax.nn.log_sigmoid(gr))
    pos = lax.broadcasted_iota(jnp.int32, gr.shape, 1) % MIX_CHUNK

    def chunk_scan(x, op):
        k = 1
        while k < MIX_CHUNK:
            x = jnp.where(pos >= k, op(x, pltpu.roll(x, k, axis=1)), x)
            k *= 2
        return x

    b = chunk_scan(gr, jnp.add)
    beta = pltpu.roll(gr, A_HEADS, axis=0) - b
    ng = gr.shape[0]
    gr_ref[0:ng, :] = beta
    gr_ref[ng:2 * ng, :] = b
    gr_ref[2 * ng:3 * ng, :] = chunk_scan(beta, jnp.maximum)

    for r0 in range(0, tm, SUB_TILE):
        rows = slice(r0, r0 + SUB_TILE)
        xn = xn_all[rows, :]
        qk = _dot(xn, wqk_ref[...])
        xp_ref[SUBLANES + r0:SUBLANES + r0 + SUB_TILE, :] = qk
        y = cw[A_CONV - 1:A_CONV, :] * qk
        for j in range(A_CONV - 1):
            y = y + cw[j:j + 1, :] * xp_ref[pl.ds(r0 + SUBLANES - (A_CONV - 1) + j, SUB_TILE), :]
        y = jax.nn.silu(y)
        q_ref[rows, :] = (y[:, :nqk // 2] * (dk ** -0.5)).astype(BF16)
        k_ref[rows, :] = y[:, nqk // 2:].astype(BF16)
        v_ref[rows, :] = _dot(xn, wv_ref[...]).astype(BF16)
        og_ref[rows, :] = jax.nn.sigmoid(_dot(xn, wo_ref[...])).astype(BF16)
    xp_ref[0:SUBLANES, :] = xp_ref[tm:tm + SUBLANES, :]


def _mlstm_recur(q_ref, k_ref, v_ref, og_ref, gr_ref, ng_ref, wout_ref, h_ref, o_ref,
                 cn_ref, m_ref, y_ref):
    ts = q_ref.shape[0]
    dk = q_ref.shape[1] // A_HEADS
    dv = v_ref.shape[1] // A_HEADS
    L = MIX_CHUNK
    H = A_HEADS
    causal = lax.broadcasted_iota(jnp.int32, (L, L), 1) <= lax.broadcasted_iota(jnp.int32, (L, L), 0)
    ones = jnp.ones((L, LANES), BF16)

    for c in range(ts // L):
        rows = slice(c * L, (c + 1) * L)
        beta = gr_ref[0:2 * H, rows]
        b_rows = gr_ref[2 * H:4 * H, rows]
        m_prev = m_ref[:, 0:1]
        big_m = jnp.maximum(m_prev, gr_ref[4 * H:6 * H, rows])
        m_last = big_m[:, L - 1:L]
        decay_all = jnp.exp(m_prev - m_last)
        m_ref[...] = jnp.broadcast_to(b_rows[:, L - 1:L] + m_last, m_ref.shape)
        m_cols = big_m.T
        w_inter_cols = jnp.exp(m_prev - big_m).T
        exp_neg_m_cols = jnp.exp(-(b_rows + big_m)).T
        wa_cols = jnp.exp(beta - m_last).T
        for hd in range(A_HEADS):
            j = H + hd
            qh = q_ref[rows, hd * dk:(hd + 1) * dk]
            kh = k_ref[rows, hd * dk:(hd + 1) * dk]
            v1 = jnp.concatenate([v_ref[rows, hd * dv:(hd + 1) * dv], ones], axis=1)
            cn_prev = cn_ref[hd]
            w_intra = jnp.exp(jnp.where(causal, beta[j:j + 1, :] - m_cols[:, j:j + 1], -jnp.inf))
            w_inter = w_inter_cols[:, j:j + 1]
            s = (_dot_nt(qh, kh) * w_intra).astype(BF16)
            both = _dot(s, v1) + w_inter * _dot(qh, cn_prev.astype(BF16))
            inv = 1.0 / jnp.maximum(jnp.abs(both[:, dv:]), exp_neg_m_cols[:, j:j + 1])
            hh = both[:, :dv] * jnp.concatenate([inv] * (dv // LANES), axis=1)
            kw = (kh.astype(F32) * wa_cols[:, j:j + 1]).astype(BF16)
            cn_ref[hd] = decay_all[j:j + 1, :] * cn_prev + _dot_tn(kw, v1)
            cols = slice(hd * dv, (hd + 1) * dv)
            yh = _head_rms(hh, ng_ref[:, cols]) * og_ref[rows, cols].astype(F32)
            y_ref[rows, cols] = yh.astype(BF16)

    o_ref[...] = h_ref[...] + _dot(y_ref[...], wout_ref[...])


def _mlstm_in_kernel(h_ref, g_ref, wqk_ref, wv_ref, wo_ref, wgt_ref, cw_ref, gb_ref,
                     q_ref, k_ref, v_ref, og_ref, gr_ref, xp_ref, *, tiles_per_seq):
    @pl.when(pl.program_id(0) % tiles_per_seq == 0)
    def _():
        xp_ref[0:SUBLANES, :] = jnp.zeros((SUBLANES, xp_ref.shape[1]), F32)

    _mlstm_project(h_ref, g_ref, wqk_ref, wv_ref, wo_ref, wgt_ref, cw_ref, gb_ref,
                   q_ref, k_ref, v_ref, og_ref, gr_ref, xp_ref)


def _mlstm_scan_kernel(q_ref, k_ref, v_ref, og_ref, gr_ref, ng_ref, wout_ref, h_ref, o_ref,
                       cn_ref, m_ref, y_ref):
    @pl.when(pl.program_id(1) == 0)
    def _():
        cn_ref[...] = jnp.zeros(cn_ref.shape, F32)
        m_ref[...] = jnp.zeros(m_ref.shape, F32)

    _mlstm_recur(q_ref, k_ref, v_ref, og_ref, gr_ref, ng_ref, wout_ref, h_ref, o_ref,
                 cn_ref, m_ref, y_ref)


def _mlstm_layer(h, gain, w_in, conv_w, gate_b, norm_g, w_out, batch):
    rows, d = h.shape
    seq = rows // batch
    nqk = conv_w.shape[1]
    dk = nqk // (2 * A_HEADS)
    dv = d // A_HEADS
    nv = A_HEADS * dv
    ng = 2 * A_HEADS
    wqk = w_in[:, :nqk].astype(BF16)
    wv = w_in[:, nqk:nqk + nv].astype(BF16)
    wo = w_in[:, nqk + nv:nqk + 2 * nv].astype(BF16)
    wgt = w_in[:, nqk + 2 * nv:].T.astype(BF16)
    tm = ROW_TILE
    q, k, v, og, gr = pl.pallas_call(
        functools.partial(_mlstm_in_kernel, tiles_per_seq=seq // tm),
        grid=(rows // tm,),
        in_specs=[pl.BlockSpec((tm, d), lambda i: (i, 0)), _whole((1, d)), _whole((d, nqk)),
                  _whole((d, nv)), _whole((d, nv)), _whole((ng, d)), _whole((A_CONV, nqk)),
                  _whole((ng, 1))],
        out_specs=[pl.BlockSpec((tm, nqk // 2), lambda i: (i, 0)),
                   pl.BlockSpec((tm, nqk // 2), lambda i: (i, 0)),
                   pl.BlockSpec((tm, nv), lambda i: (i, 0)),
                   pl.BlockSpec((tm, nv), lambda i: (i, 0)),
                   pl.BlockSpec((3 * ng, tm), lambda i: (0, i))],
        out_shape=[jax.ShapeDtypeStruct((rows, nqk // 2), BF16),
                   jax.ShapeDtypeStruct((rows, nqk // 2), BF16),
                   jax.ShapeDtypeStruct((rows, nv), BF16),
                   jax.ShapeDtypeStruct((rows, nv), BF16),
                   jax.ShapeDtypeStruct((3 * ng, rows), F32)],
        scratch_shapes=[pltpu.VMEM((tm + SUBLANES, nqk), F32)],
        compiler_params=_params(("arbitrary",)),
        name="mlstm_in",
    )(h, gain, wqk, wv, wo, wgt, conv_w, gate_b.reshape(ng, 1))

    ts = MLSTM_TILE
    per_batch = seq // ts
    tile = lambda b, i: (b * per_batch + i, 0)
    return pl.pallas_call(
        _mlstm_scan_kernel,
        grid=(batch, per_batch),
        in_specs=[pl.BlockSpec((ts, nqk // 2), tile), pl.BlockSpec((ts, nqk // 2), tile),
                  pl.BlockSpec((ts, nv), tile), pl.BlockSpec((ts, nv), tile),
                  pl.BlockSpec((3 * ng, ts), lambda b, i: (0, b * per_batch + i)),
                  _whole((1, nv)), _whole((nv, d)), pl.BlockSpec((ts, d), tile)],
        out_specs=pl.BlockSpec((ts, d), tile),
        out_shape=jax.ShapeDtypeStruct((rows, d), F32),
        scratch_shapes=[pltpu.VMEM((A_HEADS, dk, dv + LANES), F32),
                        pltpu.VMEM((ng, LANES), F32),
                        pltpu.VMEM((ts, nv), BF16)],
        compiler_params=_params(("parallel", "arbitrary")),
        name="mlstm_scan",
    )(q, k, v, og, gr, norm_g.reshape(1, nv), w_out.astype(BF16), h)


def _gla_in_kernel(h_ref, g_ref, wq_ref, wk_ref, wv_ref, wr_ref, wg1_ref, wg2_ref, gb_ref,
                   q_ref, k_ref, v_ref, rg_ref, la_ref):
    dk = wq_ref.shape[1] // B_HEADS
    xn = _rms(h_ref[...], g_ref[...]).astype(BF16)
    q_ref[...] = (_dot(xn, wq_ref[...]) * (dk ** -0.5)).astype(BF16)
    k_ref[...] = _dot(xn, wk_ref[...]).astype(BF16)
    v_ref[...] = _dot(xn, wv_ref[...]).astype(BF16)
    rg_ref[...] = jax.nn.silu(_dot(xn, wr_ref[...])).astype(BF16)
    g_low = _dot(xn, wg1_ref[...]).astype(BF16)
    la_ref[...] = jax.nn.log_sigmoid(_dot(g_low, wg2_ref[...]) + gb_ref[...]) / B_TAU


def _bcast_sublane(x3, idx):
    return jnp.broadcast_to(x3[:, idx:idx + 1, :], x3.shape)


def _boundary_rows(b, half):
    L, w = b.shape
    if half >= SUBLANES:
        nb = L // (2 * half)
        b4 = b.reshape(nb, 2 * half // SUBLANES, SUBLANES, w)
        last = b4[:, half // SUBLANES - 1, SUBLANES - 1:SUBLANES, :]
        return jnp.broadcast_to(last[:, None, :, :], b4.shape).reshape(L, w)
    b3 = b.reshape(L // SUBLANES, SUBLANES, w)
    sub = lax.broadcasted_iota(jnp.int32, b3.shape, 1)
    out = None
    for blk in range(SUBLANES // (2 * half)):
        cand = _bcast_sublane(b3, blk * 2 * half + half - 1)
        out = cand if out is None else jnp.where(sub >= blk * 2 * half, cand, out)
    return out.reshape(L, w)


def _gla_scan_kernel(q_ref, k_ref, v_ref, rg_ref, la_ref, ng_ref, wout_ref, h_ref, o_ref,
                     st_ref, y_ref):
    ts = q_ref.shape[0]
    dk = q_ref.shape[1] // B_HEADS
    dv = v_ref.shape[1] // B_HEADS
    L = MIX_CHUNK

    @pl.when(pl.program_id(1) == 0)
    def _():
        st_ref[...] = jnp.zeros(st_ref.shape, F32)

    row = lax.broadcasted_iota(jnp.int32, (L, L), 0)
    col = lax.broadcasted_iota(jnp.int32, (L, L), 1)
    tri = (col <= row).astype(BF16)
    eye = col == row
    eye_k = (lax.broadcasted_iota(jnp.int32, (dk, dk), 0) == lax.broadcasted_iota(jnp.int32, (dk, dk), 1))
    pos = lax.broadcasted_iota(jnp.int32, (L, dk), 0)

    levels = []
    half = L // 2
    while half >= 1:
        levels.append((half, (pos & half) != 0, (row // (2 * half)) == (col // (2 * half))))
        half //= 2

    for c in range(ts // L):
        rows = slice(c * L, (c + 1) * L)
        b_all = _dot_exact_lhs01(tri, la_ref[rows, :])
        for hd in range(B_HEADS):
            kc = slice(hd * dk, (hd + 1) * dk)
            vc = slice(hd * dv, (hd + 1) * dv)
            b = b_all[:, kc]
            qh = q_ref[rows, kc].astype(F32)
            kh = k_ref[rows, kc].astype(F32)
            vh = v_ref[rows, vc]
            s_mat = jnp.where(eye, _dot_nt(qh.astype(BF16), kh.astype(BF16)), 0.0)
            for half, late, same_block in levels:
                e = jnp.exp(-jnp.abs(b - _boundary_rows(b, half)))
                qt = jnp.where(late, qh * e, 0.0).astype(BF16)
                kt = jnp.where(late, 0.0, kh * e).astype(BF16)
                s_mat = s_mat + jnp.where(same_block, _dot_nt(qt, kt), 0.0)
            state = st_ref[hd]
            o = _dot(s_mat.astype(BF16), vh) + _dot((qh * jnp.exp(b)).astype(BF16), state.astype(BF16))
            b_last = b[L - 1:L, :]
            k_dec = kh * jnp.exp(b_last - b)
            decay_col = jnp.sum(jnp.where(eye_k, jnp.exp(b_last), 0.0), axis=1, keepdims=True)
            st_ref[hd] = decay_col * state + _dot_tn(k_dec.astype(BF16), vh)
            yh = _head_rms(o, ng_ref[:, vc]) * rg_ref[rows, vc].astype(F32)
            y_ref[rows, vc] = yh.astype(BF16)

    o_ref[...] = h_ref[...] + _dot(y_ref[...], wout_ref[...])


def _gla_layer(h, gain, w_in, gate_w2, gate_b, norm_g, w_out, batch):
    rows, d = h.shape
    seq = rows // batch
    nk = gate_w2.shape[1]
    dk = nk // B_HEADS
    dv = d // B_HEADS
    nv = B_HEADS * dv
    wq = w_in[:, :nk].astype(BF16)
    wk = w_in[:, nk:2 * nk].astype(BF16)
    wv = w_in[:, 2 * nk:2 * nk + nv].astype(BF16)
    wr = w_in[:, 2 * nk + nv:2 * nk + 2 * nv].astype(BF16)
    wg1 = jnp.pad(w_in[:, 2 * nk + 2 * nv:], ((0, 0), (0, LANES - B_RANK))).astype(BF16)
    wg2 = jnp.pad(gate_w2, ((0, LANES - B_RANK), (0, 0))).astype(BF16)
    tm = ROW_TILE
    row_tile = lambda i: (i, 0)
    q, k, v, rg, la = pl.pallas_call(
        _gla_in_kernel,
        grid=(rows // tm,),
        in_specs=[pl.BlockSpec((tm, d), row_tile), _whole((1, d)), _whole((d, nk)), _whole((d, nk)),
                  _whole((d, nv)), _whole((d, nv)), _whole((d, LANES)), _whole((LANES, nk)),
                  _whole((1, nk))],
        out_specs=[pl.BlockSpec((tm, nk), row_tile), pl.BlockSpec((tm, nk), row_tile),
                   pl.BlockSpec((tm, nv), row_tile), pl.BlockSpec((tm, nv), row_tile),
                   pl.BlockSpec((tm, nk), row_tile)],
        out_shape=[jax.ShapeDtypeStruct((rows, nk), BF16), jax.ShapeDtypeStruct((rows, nk), BF16),
                   jax.ShapeDtypeStruct((rows, nv), BF16), jax.ShapeDtypeStruct((rows, nv), BF16),
                   jax.ShapeDtypeStruct((rows, nk), F32)],
        compiler_params=_params(("parallel",)),
        name="gla_in",
    )(h, gain, wq, wk, wv, wr, wg1, wg2, gate_b.reshape(1, nk))

    ts = GLA_TILE
    per_batch = seq // ts
    tile = lambda b, i: (b * per_batch + i, 0)
    return pl.pallas_call(
        _gla_scan_kernel,
        grid=(batch, per_batch),
        in_specs=[pl.BlockSpec((ts, nk), tile), pl.BlockSpec((ts, nk), tile),
                  pl.BlockSpec((ts, nv), tile), pl.BlockSpec((ts, nv), tile),
                  pl.BlockSpec((ts, nk), tile), _whole((1, nv)), _whole((nv, d)),
                  pl.BlockSpec((ts, d), tile)],
        out_specs=pl.BlockSpec((ts, d), tile),
        out_shape=jax.ShapeDtypeStruct((rows, d), F32),
        scratch_shapes=[pltpu.VMEM((B_HEADS, dk, dv), F32), pltpu.VMEM((ts, nv), BF16)],
        compiler_params=_params(("parallel", "arbitrary")),
        name="gla_scan",
    )(q, k, v, rg, la, norm_g.reshape(1, nv), w_out.astype(BF16), h)


def _s5_kernel(h_ref, g_ref, win_ref, perm_ref, permt_ref, bmat_ref, acoef_ref, apow_ref, atile_ref,
               cmat_ref, dskip_ref, wg_ref, bg_ref, wo_ref, o_ref, carry_ref, bu_ref, s_ref, sb_ref, yp_ref):
    ts, d = h_ref.shape
    nblk = bmat_ref.shape[0]
    half = bmat_ref.shape[2] // 2
    nj = ts // SUBLANES

    @pl.when(pl.program_id(1) == 0)
    def _():
        carry_ref[...] = jnp.zeros(carry_ref.shape, F32)

    xn = _rms(h_ref[...], g_ref[...]).astype(BF16)
    xp = _dot(perm_ref[...], xn).astype(BF16)
    u = _dot(xp, win_ref[...])
    ub = u.astype(BF16)

    for blk in range(nblk):
        bu_ref[...] = _dot(ub[:, blk * LANES:(blk + 1) * LANES], bmat_ref[blk])
        a_re = acoef_ref[blk, 0]
        a_im = acoef_ref[blk, 1]

        def local_step(j, x):
            x_re, x_im = x
            r = pl.ds(pl.multiple_of(j * SUBLANES, SUBLANES), SUBLANES)
            n_re = a_re * x_re - a_im * x_im + bu_ref[r, 0:half]
            n_im = a_re * x_im + a_im * x_re + bu_ref[r, half:2 * half]
            s_ref[r, 0:half] = n_re
            s_ref[r, half:2 * half] = n_im
            return n_re, n_im

        zero = jnp.zeros((SUBLANES, half), F32)
        end_re, end_im = lax.fori_loop(0, nj, local_step, (zero, zero), unroll=S5_UNROLL)

        t_re = atile_ref[blk, 0]
        t_im = atile_ref[blk, 1]
        c_re = carry_ref[blk, 0:1, 0:half]
        c_im = carry_ref[blk, 0:1, half:2 * half]
        cin_re, cin_im = [], []
        for r in range(SUBLANES):
            cin_re.append(c_re)
            cin_im.append(c_im)
            e_re = end_re[r:r + 1, :] + t_re * c_re - t_im * c_im
            e_im = end_im[r:r + 1, :] + t_re * c_im + t_im * c_re
            c_re, c_im = e_re, e_im
        carry_ref[blk, :, 0:half] = jnp.broadcast_to(c_re, (SUBLANES, half))
        carry_ref[blk, :, half:2 * half] = jnp.broadcast_to(c_im, (SUBLANES, half))
        cin_re = jnp.concatenate(cin_re, axis=0)
        cin_im = jnp.concatenate(cin_im, axis=0)

        def fix_step(jj, _):
            out_re, out_im = [], []
            for k in range(2):
                j = 2 * jj + k
                r = pl.ds(pl.multiple_of(j * SUBLANES, SUBLANES), SUBLANES)
                p_re = apow_ref[blk, 0, pl.ds(j, 1), :]
                p_im = apow_ref[blk, 1, pl.ds(j, 1), :]
                out_re.append(s_ref[r, 0:half] + p_re * cin_re - p_im * cin_im)
                out_im.append(s_ref[r, half:2 * half] + p_re * cin_im + p_im * cin_re)
            r2 = pl.ds(pl.multiple_of(jj * 2 * SUBLANES, 2 * SUBLANES), 2 * SUBLANES)
            sb_ref[r2, 0:half] = jnp.concatenate(out_re, axis=0).astype(BF16)
            sb_ref[r2, half:2 * half] = jnp.concatenate(out_im, axis=0).astype(BF16)
            return 0

        lax.fori_loop(0, nj // 2, fix_step, 0, unroll=S5_UNROLL // 2)
        yp_ref[:, blk * LANES:(blk + 1) * LANES] = _dot(sb_ref[...], cmat_ref[blk])

    y = jax.nn.gelu(yp_ref[...] + dskip_ref[...] * u)
    z = y * jax.nn.sigmoid(_dot(y.astype(BF16), wg_ref[...]) + bg_ref[...])
    z = _dot(permt_ref[...], z.astype(BF16)).astype(BF16)
    o_ref[...] = h_ref[...] + _dot(z, wo_ref[...])


def _s5_discretise(lam_re, lam_im, log_dt, b_re, b_im, c_re, c_im, ts):
    groups, states = lam_re.shape
    gpb = S5_BLOCK_GROUPS
    nblk = groups // gpb
    lr = lam_re.astype(F32)
    li = lam_im.astype(F32)
    dt = jnp.exp(log_dt.astype(F32))[:, None]
    mag = jnp.exp(lr * dt)
    ab_re = mag * jnp.cos(li * dt)
    ab_im = mag * jnp.sin(li * dt)
    den = lr * lr + li * li
    z_re = ((ab_re - 1.0) * lr + ab_im * li) / den
    z_im = (ab_im * lr - (ab_re - 1.0) * li) / den
    br = b_re.astype(F32)
    bi = b_im.astype(F32)
    bb_re = z_re[..., None] * br - z_im[..., None] * bi
    bb_im = z_re[..., None] * bi + z_im[..., None] * br
    eye = jnp.eye(gpb, dtype=F32)

    def in_block(bb):
        t = bb.reshape(nblk, gpb, states, C_GROUP)
        return jnp.einsum('ngph,gk->nghkp', t, eye).reshape(nblk, gpb * C_GROUP, gpb * states)

    bmat = jnp.concatenate([in_block(bb_re), in_block(bb_im)], axis=-1).astype(BF16)

    def out_block(cc):
        t = cc.astype(F32).reshape(nblk, gpb, C_GROUP, states)
        return jnp.einsum('nghp,gk->ngpkh', t, eye).reshape(nblk, gpb * states, gpb * C_GROUP)

    cmat = jnp.concatenate([out_block(c_re), -out_block(c_im)], axis=1).astype(BF16)

    def powers(n):
        n = n.astype(F32)[:, None, None]
        m = jnp.exp(n * (lr * dt)[None])
        ang = n * (li * dt)[None]
        pr = (m * jnp.cos(ang)).reshape(-1, nblk, gpb * states).transpose(1, 0, 2)
        pi = (m * jnp.sin(ang)).reshape(-1, nblk, gpb * states).transpose(1, 0, 2)
        return jnp.stack([pr, pi], axis=1)

    nj = ts // SUBLANES
    acoef = jnp.broadcast_to(powers(jnp.ones((1,)))[:, :, 0:1, :], (nblk, 2, SUBLANES, gpb * states))
    apow = powers(jnp.arange(1, nj + 1))
    atile = powers(jnp.full((1,), nj))
    return bmat, cmat, acoef, apow, atile


def _s5_layer(h, gain, w_in, lam_re, lam_im, log_dt, b_re, b_im, c_re, c_im, d_skip, w_gate, b_gate,
              w_out, batch):
    rows, d = h.shape
    seq = rows // batch
    ts = S5_TILE
    nj = ts // SUBLANES
    per_batch = seq // ts
    bmat, cmat, acoef, apow, atile = _s5_discretise(lam_re, lam_im, log_dt, b_re, b_im, c_re, c_im, ts)
    nblk, _, sw = bmat.shape
    p = jnp.arange(ts)
    perm = (p[None, :] == ((p % SUBLANES) * nj + p // SUBLANES)[:, None]).astype(BF16)
    tile = lambda b, i: (b * per_batch + i, 0)
    return pl.pallas_call(
        _s5_kernel,
        grid=(batch, per_batch),
        in_specs=[pl.BlockSpec((ts, d), tile), _whole((1, d)), _whole((d, d)), _whole((ts, ts)), _whole((ts, ts)),
                  _whole(bmat.shape), _whole(acoef.shape), _whole(apow.shape), _whole(atile.shape),
                  _whole(cmat.shape), _whole((1, d)), _whole((d, d)), _whole((1, d)), _whole((d, d))],
        out_specs=pl.BlockSpec((ts, d), tile),
        out_shape=jax.ShapeDtypeStruct((rows, d), F32),
        scratch_shapes=[pltpu.VMEM((nblk, SUBLANES, sw), F32), pltpu.VMEM((ts, sw), F32),
                        pltpu.VMEM((ts, sw), F32), pltpu.VMEM((ts, sw), BF16),
                        pltpu.VMEM((ts, d), F32)],
        compiler_params=_params(("parallel", "arbitrary")),
        name="s5_mix",
    )(h, gain, w_in.astype(BF16), perm, perm.T, bmat, acoef, apow, atile, cmat, d_skip.reshape(1, d),
      w_gate.astype(BF16), b_gate.reshape(1, d), w_out.astype(BF16))


def kernel(x, mem, norm_mix, norm_x, norm_ffn, norm_mem, norm_final, a_w_in, a_conv_w, a_gate_b, a_norm, a_w_out, b_w_in, b_gate_w2, b_gate_b, b_norm, b_w_out, c_w_in, c_lam_re, c_lam_im, c_log_dt, c_b_re, c_b_im, c_c_re, c_c_im, c_d, c_w_gate, c_b_gate, c_w_out, x_w_q, x_w_kv, x_w_o, f_w_gate, f_w_up, f_w_down):
    batch, seq, d = x.shape
    mem_len = mem.shape[1]
    depth = norm_mix.shape[0]
    assert seq % ROW_TILE == 0 and seq % MLSTM_TILE == 0 and seq % GLA_TILE == 0 and seq % S5_TILE == 0
    assert MLSTM_TILE % MIX_CHUNK == 0 and GLA_TILE % MIX_CHUNK == 0
    assert f_w_gate.shape[2] % FF_CHUNK == 0 and d % (X_HEADS * LANES) == 0

    kv = _mem_kv(mem.reshape(batch * mem_len, d), norm_mem.reshape(1, d), x_w_kv.astype(BF16), mem_len)

    h = x.reshape(batch * seq, d)
    for i in range(depth):
        kind = i % N_MIXERS
        j = i // N_MIXERS
        gain = norm_mix[i].reshape(1, d)
        if kind == 0:
            h = _mlstm_layer(h, gain, a_w_in[j], a_conv_w[j], a_gate_b[j], a_norm[j], a_w_out[j], batch)
        elif kind == 1:
            h = _gla_layer(h, gain, b_w_in[j], b_gate_w2[j], b_gate_b[j], b_norm[j], b_w_out[j], batch)
        else:
            h = _s5_layer(h, gain, c_w_in[j], c_lam_re[j], c_lam_im[j], c_log_dt[j], c_b_re[j], c_b_im[j],
                          c_c_re[j], c_c_im[j], c_d[j], c_w_gate[j], c_b_gate[j], c_w_out[j], batch)
        h = _xattn(h, norm_x[i].reshape(1, d), x_w_q[i].astype(BF16), kv, i, x_w_o[i].astype(BF16),
                   batch, mem_len)
        h = _swiglu(h, norm_ffn[i].reshape(1, d), f_w_gate[i].astype(BF16), f_w_up[i].astype(BF16),
                    f_w_down[i].astype(BF16), norm_final.reshape(1, d), final_norm=(i == depth - 1))
    return h.reshape(batch, seq, d)
```

```python
import functools
import math

import jax
import jax.numpy as jnp
from jax import lax
from jax.experimental import pallas as pl
from jax.experimental.pallas import tpu as pltpu

F32 = jnp.float32
BF16 = jnp.bfloat16
EPS = 1e-6

N_MIXERS = 3
A_HEADS = 4
A_CONV = 4
B_HEADS = 4
B_RANK = 16
B_TAU = 16.0
C_GROUP = 16
C_STATE = 64
X_HEADS = 4

LANES = 128
SUBLANES = 8
VMEM_LIMIT_BYTES = 56 * 1024 * 1024

ROW_TILE = 512
SUB_TILE = 256
MLSTM_IN_TILE = 1024
MLSTM_TILE = 1024
GLA_TILE = 1024
MIX_CHUNK = 512
GLA_CHUNK = 128
S5_TILE = 256
S5_UNROLL = 32
FF_CHUNK = 256
S5_BLOCK_GROUPS = LANES // C_GROUP


def _params(semantics):
    return pltpu.CompilerParams(dimension_semantics=semantics, vmem_limit_bytes=VMEM_LIMIT_BYTES)


def _whole(shape):
    zeros = (0,) * len(shape)
    return pl.BlockSpec(shape, lambda *_: zeros)


def _rms(x, g):
    return x * lax.rsqrt(jnp.mean(x * x, axis=-1, keepdims=True) + EPS) * g


def _dot(a, b):
    return jnp.dot(a, b, preferred_element_type=F32)


def _dot_nt(a, b):
    return lax.dot_general(a, b, (((1,), (1,)), ((), ())), preferred_element_type=F32)


def _dot_tn(a, b):
    return lax.dot_general(a, b, (((0,), (0,)), ((), ())), preferred_element_type=F32)


def _split3(x):
    hi = x.astype(BF16)
    r = x - hi.astype(F32)
    mid = r.astype(BF16)
    lo = (r - mid.astype(F32)).astype(BF16)
    return hi, mid, lo


def _dot_exact_lhs01(m01, x):
    hi, mid, lo = _split3(x)
    return _dot(m01, hi) + _dot(m01, mid) + _dot(m01, lo)


def _head_rms(o, g):
    return o * lax.rsqrt(jnp.mean(o * o, axis=-1, keepdims=True) + EPS) * g


def _mem_kv_kernel(x_ref, g_ref, w_ref, o_ref):
    xn = _rms(x_ref[...], g_ref[...]).astype(BF16)
    o_ref[...] = _dot(xn, w_ref[0]).astype(BF16)


def _mem_kv(x, g, w, tm):
    rows, d = x.shape
    depth, _, n = w.shape
    return pl.pallas_call(
        _mem_kv_kernel,
        grid=(depth, rows // tm),
        in_specs=[pl.BlockSpec((tm, d), lambda l, i: (i, 0)), _whole((1, d)),
                  pl.BlockSpec((1, d, n), lambda l, i: (l, 0, 0))],
        out_specs=pl.BlockSpec((tm, n), lambda l, i: (i, l)),
        out_shape=jax.ShapeDtypeStruct((rows, depth * n), BF16),
        compiler_params=_params(("parallel", "parallel")),
        name="mem_kv_proj",
    )(x, g, w)


def _xattn_kernel(h_ref, g_ref, wq_ref, kv_ref, wo_ref, o_ref):
    d = h_ref.shape[1]
    dh = d // X_HEADS
    x = h_ref[...]
    xn = _rms(x, g_ref[...]).astype(BF16)
    q = _dot(xn, wq_ref[...]).astype(BF16)
    outs = []
    for hd in range(X_HEADS):
        qh = q[:, hd * dh:(hd + 1) * dh]
        kh = kv_ref[:, hd * dh:(hd + 1) * dh]
        vh = kv_ref[:, d + hd * dh:d + (hd + 1) * dh]
        s = _dot_nt(qh, kh) * (dh ** -0.5)
        e = jnp.exp(s - jnp.max(s, axis=-1, keepdims=True))
        p = e / jnp.sum(e, axis=-1, keepdims=True)
        outs.append(_dot(p.astype(BF16), vh))
    o = jnp.concatenate(outs, axis=-1).astype(BF16)
    o_ref[...] = x + _dot(o, wo_ref[...])


def _xattn(h, g, wq, kv, layer, wo, batch, mem_len):
    rows, d = h.shape
    tm = ROW_TILE
    per_batch = rows // batch // tm
    return pl.pallas_call(
        _xattn_kernel,
        grid=(batch, per_batch),
        in_specs=[
            pl.BlockSpec((tm, d), lambda b, i: (b * per_batch + i, 0)),
            _whole((1, d)),
            _whole((d, d)),
            pl.BlockSpec((mem_len, 2 * d), lambda b, i: (b, layer)),
            _whole((d, d)),
        ],
        out_specs=pl.BlockSpec((tm, d), lambda b, i: (b * per_batch + i, 0)),
        out_shape=jax.ShapeDtypeStruct((rows, d), F32),
        compiler_params=_params(("parallel", "parallel")),
        name="xattn",
    )(h, g, wq, kv, wo)


def _swiglu_kernel(h_ref, g_ref, wg_ref, wu_ref, wd_ref, gf_ref, o_ref, *, final_norm):
    x = h_ref[...]
    xn = _rms(x, g_ref[...]).astype(BF16)
    o_ref[...] = x
    for c in range(wg_ref.shape[1] // FF_CHUNK):
        cols = slice(c * FF_CHUNK, (c + 1) * FF_CHUNK)
        gate = _dot(xn, wg_ref[:, cols])
        up = _dot(xn, wu_ref[:, cols])
        act = (jax.nn.silu(gate) * up).astype(BF16)
        o_ref[...] += _dot(act, wd_ref[cols, :])
    if final_norm:
        o_ref[...] = _rms(o_ref[...], gf_ref[...])


def _swiglu(h, g, wg, wu, wd, gf, final_norm):
    rows, d = h.shape
    tm = ROW_TILE
    return pl.pallas_call(
        functools.partial(_swiglu_kernel, final_norm=final_norm),
        grid=(rows // tm,),
        in_specs=[pl.BlockSpec((tm, d), lambda i: (i, 0)), _whole((1, d)), _whole(wg.shape),
                  _whole(wu.shape), _whole(wd.shape), _whole((1, d))],
        out_specs=pl.BlockSpec((tm, d), lambda i: (i, 0)),
        out_shape=jax.ShapeDtypeStruct((rows, d), F32),
        compiler_params=_params(("parallel",)),
        name="swiglu",
    )(h, g, wg, wu, wd, gf)


def _mlstm_project(h_ref, g_ref, wqk_ref, wv_ref, wo_ref, wgt_ref, cw_ref, gb_ref,
                   q_ref, k_ref, v_ref, og_ref, gr_ref, xp_ref):
    tm, d = h_ref.shape
    nqk = wqk_ref.shape[1]
    dk = nqk // (2 * A_HEADS)
    cw = cw_ref[...]
    xn_all = _rms(h_ref[...], g_ref[...]).astype(BF16)

    gr = _dot_nt(wgt_ref[...], xn_all) + gb_ref[...]
    is_input_gate = lax.broadcasted_iota(jnp.int32, gr.shape, 0) < A_HEADS
    gr = jnp.where(is_input_gate, gr, jax.nn.log_sigmoid(gr))
    pos = lax.broadcasted_iota(jnp.int32, gr.shape, 1) % MIX_CHUNK

    def chunk_scan(x, op):
        k = 1
        while k < MIX_CHUNK:
            x = jnp.where(pos >= k, op(x, pltpu.roll(x, k, axis=1)), x)
            k *= 2
        return x

    b = chunk_scan(gr, jnp.add)
    beta = pltpu.roll(gr, A_HEADS, axis=0) - b
    ng = gr.shape[0]
    gr_ref[0:ng, :] = beta
    gr_ref[ng:2 * ng, :] = b
    gr_ref[2 * ng:3 * ng, :] = chunk_scan(beta, jnp.maximum)

    for r0 in range(0, tm, SUB_TILE):
        rows = slice(r0, r0 + SUB_TILE)
        xn = xn_all[rows, :]
        qk = _dot(xn, wqk_ref[...])
        xp_ref[SUBLANES + r0:SUBLANES + r0 + SUB_TILE, :] = qk
        y = cw[A_CONV - 1:A_CONV, :] * qk
        for j in range(A_CONV - 1):
            y = y + cw[j:j + 1, :] * xp_ref[pl.ds(r0 + SUBLANES - (A_CONV - 1) + j, SUB_TILE), :]
        y = jax.nn.silu(y)
        q_ref[rows, :] = (y[:, :nqk // 2] * (dk ** -0.5)).astype(BF16)
        k_ref[rows, :] = y[:, nqk // 2:].astype(BF16)
        v_ref[rows, :] = _dot(xn, wv_ref[...]).astype(BF16)
        og_ref[rows, :] = jax.nn.sigmoid(_dot(xn, wo_ref[...])).astype(BF16)
    xp_ref[0:SUBLANES, :] = xp_ref[tm:tm + SUBLANES, :]


def _mlstm_recur(q_ref, k_ref, v_ref, og_ref, gr_ref, ng_ref, wout_ref, h_ref, o_ref,
                 cn_ref, m_ref, y_ref):
    ts = q_ref.shape[0]
    dk = q_ref.shape[1] // A_HEADS
    dv = v_ref.shape[1] // A_HEADS
    L = MIX_CHUNK
    H = A_HEADS
    causal = lax.broadcasted_iota(jnp.int32, (L, L), 1) <= lax.broadcasted_iota(jnp.int32, (L, L), 0)
    ones = jnp.ones((L, LANES), BF16)

    for c in range(ts // L):
        rows = slice(c * L, (c + 1) * L)
        beta = gr_ref[0:2 * H, rows]
        b_rows = gr_ref[2 * H:4 * H, rows]
        m_prev = m_ref[:, 0:1]
        big_m = jnp.maximum(m_prev, gr_ref[4 * H:6 * H, rows])
        m_last = big_m[:, L - 1:L]
        decay_all = jnp.exp(m_prev - m_last)
        m_ref[...] = jnp.broadcast_to(b_rows[:, L - 1:L] + m_last, m_ref.shape)
        m_cols = big_m.T
        w_inter_cols = jnp.exp(m_prev - big_m).T
        exp_neg_m_cols = jnp.exp(-(b_rows + big_m)).T
        wa_cols = jnp.exp(beta - m_last).T
        for hd in range(A_HEADS):
            j = H + hd
            qh = q_ref[rows, hd * dk:(hd + 1) * dk]
            kh = k_ref[rows, hd * dk:(hd + 1) * dk]
            v1 = jnp.concatenate([v_ref[rows, hd * dv:(hd + 1) * dv], ones], axis=1)
            cn_prev = cn_ref[hd]
            w_intra = jnp.exp(jnp.where(causal, beta[j:j + 1, :] - m_cols[:, j:j + 1], -jnp.inf))
            w_inter = w_inter_cols[:, j:j + 1]
            s = (_dot_nt(qh, kh) * w_intra).astype(BF16)
            both = _dot(s, v1) + w_inter * _dot(qh, cn_prev.astype(BF16))
            inv = 1.0 / jnp.maximum(jnp.abs(both[:, dv:]), exp_neg_m_cols[:, j:j + 1])
            hh = both[:, :dv] * jnp.concatenate([inv] * (dv // LANES), axis=1)
            kw = (kh.astype(F32) * wa_cols[:, j:j + 1]).astype(BF16)
            cn_ref[hd] = decay_all[j:j + 1, :] * cn_prev + _dot_tn(kw, v1)
            cols = slice(hd * dv, (hd + 1) * dv)
            yh = _head_rms(hh, ng_ref[:, cols]) * og_ref[rows, cols].astype(F32)
            y_ref[rows, cols] = yh.astype(BF16)

    o_ref[...] = h_ref[...] + _dot(y_ref[...], wout_ref[...])


def _mlstm_in_kernel(h_ref, g_ref, wqk_ref, wv_ref, wo_ref, wgt_ref, cw_ref, gb_ref,
                     q_ref, k_ref, v_ref, og_ref, gr_ref, xp_ref, *, tiles_per_seq):
    @pl.when(pl.program_id(0) % tiles_per_seq == 0)
    def _():
        xp_ref[0:SUBLANES, :] = jnp.zeros((SUBLANES, xp_ref.shape[1]), F32)

    _mlstm_project(h_ref, g_ref, wqk_ref, wv_ref, wo_ref, wgt_ref, cw_ref, gb_ref,
                   q_ref, k_ref, v_ref, og_ref, gr_ref, xp_ref)


def _mlstm_scan_kernel(q_ref, k_ref, v_ref, og_ref, gr_ref, ng_ref, wout_ref, h_ref, o_ref,
                       cn_ref, m_ref, y_ref):
    @pl.when(pl.program_id(1) == 0)
    def _():
        cn_ref[...] = jnp.zeros(cn_ref.shape, F32)
        m_ref[...] = jnp.zeros(m_ref.shape, F32)

    _mlstm_recur(q_ref, k_ref, v_ref, og_ref, gr_ref, ng_ref, wout_ref, h_ref, o_ref,
                 cn_ref, m_ref, y_ref)


def _mlstm_layer(h, gain, w_in, conv_w, gate_b, norm_g, w_out, batch):
    rows, d = h.shape
    seq = rows // batch
    nqk = conv_w.shape[1]
    dk = nqk // (2 * A_HEADS)
    dv = d // A_HEADS
    nv = A_HEADS * dv
    ng = 2 * A_HEADS
    wqk = w_in[:, :nqk].astype(BF16)
    wv = w_in[:, nqk:nqk + nv].astype(BF16)
    wo = w_in[:, nqk + nv:nqk + 2 * nv].astype(BF16)
    wgt = w_in[:, nqk + 2 * nv:].T.astype(BF16)
    tm = MLSTM_IN_TILE
    q, k, v, og, gr = pl.pallas_call(
        functools.partial(_mlstm_in_kernel, tiles_per_seq=seq // tm),
        grid=(rows // tm,),
        in_specs=[pl.BlockSpec((tm, d), lambda i: (i, 0)), _whole((1, d)), _whole((d, nqk)),
                  _whole((d, nv)), _whole((d, nv)), _whole((ng, d)), _whole((A_CONV, nqk)),
                  _whole((ng, 1))],
        out_specs=[pl.BlockSpec((tm, nqk // 2), lambda i: (i, 0)),
                   pl.BlockSpec((tm, nqk // 2), lambda i: (i, 0)),
                   pl.BlockSpec((tm, nv), lambda i: (i, 0)),
                   pl.BlockSpec((tm, nv), lambda i: (i, 0)),
                   pl.BlockSpec((3 * ng, tm), lambda i: (0, i))],
        out_shape=[jax.ShapeDtypeStruct((rows, nqk // 2), BF16),
                   jax.ShapeDtypeStruct((rows, nqk // 2), BF16),
                   jax.ShapeDtypeStruct((rows, nv), BF16),
                   jax.ShapeDtypeStruct((rows, nv), BF16),
                   jax.ShapeDtypeStruct((3 * ng, rows), F32)],
        scratch_shapes=[pltpu.VMEM((tm + SUBLANES, nqk), F32)],
        compiler_params=_params(("arbitrary",)),
        name="mlstm_in",
    )(h, gain, wqk, wv, wo, wgt, conv_w, gate_b.reshape(ng, 1))

    ts = MLSTM_TILE
    per_batch = seq // ts
    tile = lambda b, i: (b * per_batch + i, 0)
    return pl.pallas_call(
        _mlstm_scan_kernel,
        grid=(batch, per_batch),
        in_specs=[pl.BlockSpec((ts, nqk // 2), tile), pl.BlockSpec((ts, nqk // 2), tile),
                  pl.BlockSpec((ts, nv), tile), pl.BlockSpec((ts, nv), tile),
                  pl.BlockSpec((3 * ng, ts), lambda b, i: (0, b * per_batch + i)),
                  _whole((1, nv)), _whole((nv, d)), pl.BlockSpec((ts, d), tile)],
        out_specs=pl.BlockSpec((ts, d), tile),
        out_shape=jax.ShapeDtypeStruct((rows, d), F32),
        scratch_shapes=[pltpu.VMEM((A_HEADS, dk, dv + LANES), F32),
                        pltpu.VMEM((ng, LANES), F32),
                        pltpu.VMEM((ts, nv), BF16)],
        compiler_params=_params(("parallel", "arbitrary")),
        name="mlstm_scan",
    )(q, k, v, og, gr, norm_g.reshape(1, nv), w_out.astype(BF16), h)


def _gla_in_kernel(h_ref, g_ref, wq_ref, wk_ref, wv_ref, wr_ref, wg1_ref, wg2_ref, gb_ref,
                   q_ref, k_ref, v_ref, rg_ref, la_ref):
    dk = wq_ref.shape[1] // B_HEADS
    xn = _rms(h_ref[...], g_ref[...]).astype(BF16)
    q_ref[...] = (_dot(xn, wq_ref[...]) * (dk ** -0.5)).astype(BF16)
    k_ref[...] = _dot(xn, wk_ref[...]).astype(BF16)
    v_ref[...] = _dot(xn, wv_ref[...]).astype(BF16)
    rg_ref[...] = jax.nn.silu(_dot(xn, wr_ref[...])).astype(BF16)
    g_low = _dot(xn, wg1_ref[...]).astype(BF16)
    la_ref[...] = jax.nn.log_sigmoid(_dot(g_low, wg2_ref[...]) + gb_ref[...]) / B_TAU


def _bcast_sublane(x3, idx):
    return jnp.broadcast_to(x3[:, idx:idx + 1, :], x3.shape)


def _boundary_rows(b, half):
    L, w = b.shape
    if half >= SUBLANES:
        nb = L // (2 * half)
        b4 = b.reshape(nb, 2 * half // SUBLANES, SUBLANES, w)
        last = b4[:, half // SUBLANES - 1, SUBLANES - 1:SUBLANES, :]
        return jnp.broadcast_to(last[:, None, :, :], b4.shape).reshape(L, w)
    b3 = b.reshape(L // SUBLANES, SUBLANES, w)
    sub = lax.broadcasted_iota(jnp.int32, b3.shape, 1)
    out = None
    for blk in range(SUBLANES // (2 * half)):
        cand = _bcast_sublane(b3, blk * 2 * half + half - 1)
        out = cand if out is None else jnp.where(sub >= blk * 2 * half, cand, out)
    return out.reshape(L, w)


def _gla_scan_kernel(q_ref, k_ref, v_ref, rg_ref, la_ref, ng_ref, wout_ref, h_ref, o_ref,
                     st_ref, y_ref):
    ts = q_ref.shape[0]
    dk = q_ref.shape[1] // B_HEADS
    dv = v_ref.shape[1] // B_HEADS
    L = GLA_CHUNK

    @pl.when(pl.program_id(1) == 0)
    def _():
        st_ref[...] = jnp.zeros(st_ref.shape, F32)

    row = lax.broadcasted_iota(jnp.int32, (L, L), 0)
    col = lax.broadcasted_iota(jnp.int32, (L, L), 1)
    tri = (col <= row).astype(BF16)
    eye = (col == row).astype(F32)
    eye_k = (lax.broadcasted_iota(jnp.int32, (dk, dk), 0) == lax.broadcasted_iota(jnp.int32, (dk, dk), 1))

    levels = []
    half = L // 2
    while half >= 1:
        pairs = ((row // (2 * half)) == (col // (2 * half))) & ((row & half) != 0) & ((col & half) == 0)
        levels.append((half, pairs.astype(F32)))
        half //= 2

    for c in range(ts // L):
        rows = slice(c * L, (c + 1) * L)
        b_all = _dot_exact_lhs01(tri, la_ref[rows, :])
        for hd in range(B_HEADS):
            kc = slice(hd * dk, (hd + 1) * dk)
            vc = slice(hd * dv, (hd + 1) * dv)
            b = b_all[:, kc]
            qh = q_ref[rows, kc].astype(F32)
            kh = k_ref[rows, kc].astype(F32)
            vh = v_ref[rows, vc]
            s_mat = eye * _dot_nt(qh.astype(BF16), kh.astype(BF16))
            for half, pairs in levels:
                e = jnp.exp(-jnp.abs(b - _boundary_rows(b, half)))
                s_mat = s_mat + pairs * _dot_nt((qh * e).astype(BF16), (kh * e).astype(BF16))
            state = st_ref[hd]
            o = _dot(s_mat.astype(BF16), vh) + _dot((qh * jnp.exp(b)).astype(BF16), state.astype(BF16))
            b_last = b[L - 1:L, :]
            k_dec = kh * jnp.exp(b_last - b)
            decay_col = jnp.sum(jnp.where(eye_k, jnp.exp(b_last), 0.0), axis=1, keepdims=True)
            st_ref[hd] = decay_col * state + _dot_tn(k_dec.astype(BF16), vh)
            yh = _head_rms(o, ng_ref[:, vc]) * rg_ref[rows, vc].astype(F32)
            y_ref[rows, vc] = yh.astype(BF16)

    o_ref[...] = h_ref[...] + _dot(y_ref[...], wout_ref[...])


def _gla_layer(h, gain, w_in, gate_w2, gate_b, norm_g, w_out, batch):
    rows, d = h.shape
    seq = rows // batch
    nk = gate_w2.shape[1]
    dk = nk // B_HEADS
    dv = d // B_HEADS
    nv = B_HEADS * dv
    wq = w_in[:, :nk].astype(BF16)
    wk = w_in[:, nk:2 * nk].astype(BF16)
    wv = w_in[:, 2 * nk:2 * nk + nv].astype(BF16)
    wr = w_in[:, 2 * nk + nv:2 * nk + 2 * nv].astype(BF16)
    wg1 = jnp.pad(w_in[:, 2 * nk + 2 * nv:], ((0, 0), (0, LANES - B_RANK))).astype(BF16)
    wg2 = jnp.pad(gate_w2, ((0, LANES - B_RANK), (0, 0))).astype(BF16)
    tm = ROW_TILE
    row_tile = lambda i: (i, 0)
    q, k, v, rg, la = pl.pallas_call(
        _gla_in_kernel,
        grid=(rows // tm,),
        in_specs=[pl.BlockSpec((tm, d), row_tile), _whole((1, d)), _whole((d, nk)), _whole((d, nk)),
                  _whole((d, nv)), _whole((d, nv)), _whole((d, LANES)), _whole((LANES, nk)),
                  _whole((1, nk))],
        out_specs=[pl.BlockSpec((tm, nk), row_tile), pl.BlockSpec((tm, nk), row_tile),
                   pl.BlockSpec((tm, nv), row_tile), pl.BlockSpec((tm, nv), row_tile),
                   pl.BlockSpec((tm, nk), row_tile)],
        out_shape=[jax.ShapeDtypeStruct((rows, nk), BF16), jax.ShapeDtypeStruct((rows, nk), BF16),
                   jax.ShapeDtypeStruct((rows, nv), BF16), jax.ShapeDtypeStruct((rows, nv), BF16),
                   jax.ShapeDtypeStruct((rows, nk), F32)],
        compiler_params=_params(("parallel",)),
        name="gla_in",
    )(h, gain, wq, wk, wv, wr, wg1, wg2, gate_b.reshape(1, nk))

    ts = GLA_TILE
    per_batch = seq // ts
    tile = lambda b, i: (b * per_batch + i, 0)
    return pl.pallas_call(
        _gla_scan_kernel,
        grid=(batch, per_batch),
        in_specs=[pl.BlockSpec((ts, nk), tile), pl.BlockSpec((ts, nk), tile),
                  pl.BlockSpec((ts, nv), tile), pl.BlockSpec((ts, nv), tile),
                  pl.BlockSpec((ts, nk), tile), _whole((1, nv)), _whole((nv, d)),
                  pl.BlockSpec((ts, d), tile)],
        out_specs=pl.BlockSpec((ts, d), tile),
        out_shape=jax.ShapeDtypeStruct((rows, d), F32),
        scratch_shapes=[pltpu.VMEM((B_HEADS, dk, dv), F32), pltpu.VMEM((ts, nv), BF16)],
        compiler_params=_params(("parallel", "arbitrary")),
        name="gla_scan",
    )(q, k, v, rg, la, norm_g.reshape(1, nv), w_out.astype(BF16), h)


def _s5_kernel(h_ref, g_ref, win_ref, perm_ref, permt_ref, bmat_ref, acoef_ref, apow_ref, atile_ref,
               cmat_ref, dskip_ref, wg_ref, bg_ref, wo_ref, o_ref, carry_ref, bu_ref, s_ref, sb_ref, yp_ref):
    ts, d = h_ref.shape
    nblk = bmat_ref.shape[0]
    half = bmat_ref.shape[2] // 2
    nj = ts // SUBLANES

    @pl.when(pl.program_id(1) == 0)
    def _():
        carry_ref[...] = jnp.zeros(carry_ref.shape, F32)

    xn = _rms(h_ref[...], g_ref[...]).astype(BF16)
    xp = _dot(perm_ref[...], xn).astype(BF16)
    u = _dot(xp, win_ref[...])
    ub = u.astype(BF16)

    for blk in range(nblk):
        bu_ref[...] = _dot(ub[:, blk * LANES:(blk + 1) * LANES], bmat_ref[blk])
        a_re = acoef_ref[blk, 0]
        a_im = acoef_ref[blk, 1]

        def local_step(j, x):
            x_re, x_im = x
            r = pl.ds(pl.multiple_of(j * SUBLANES, SUBLANES), SUBLANES)
            n_re = a_re * x_re - a_im * x_im + bu_ref[r, 0:half]
            n_im = a_re * x_im + a_im * x_re + bu_ref[r, half:2 * half]
            s_ref[r, 0:half] = n_re
            s_ref[r, half:2 * half] = n_im
            return n_re, n_im

        zero = jnp.zeros((SUBLANES, half), F32)
        end_re, end_im = lax.fori_loop(0, nj, local_step, (zero, zero), unroll=S5_UNROLL)

        t_re = atile_ref[blk, 0]
        t_im = atile_ref[blk, 1]
        c_re = carry_ref[blk, 0:1, 0:half]
        c_im = carry_ref[blk, 0:1, half:2 * half]
        cin_re, cin_im = [], []
        for r in range(SUBLANES):
            cin_re.append(c_re)
            cin_im.append(c_im)
            e_re = end_re[r:r + 1, :] + t_re * c_re - t_im * c_im
            e_im = end_im[r:r + 1, :] + t_re * c_im + t_im * c_re
            c_re, c_im = e_re, e_im
        carry_ref[blk, :, 0:half] = jnp.broadcast_to(c_re, (SUBLANES, half))
        carry_ref[blk, :, half:2 * half] = jnp.broadcast_to(c_im, (SUBLANES, half))
        cin_re = jnp.concatenate(cin_re, axis=0)
        cin_im = jnp.concatenate(cin_im, axis=0)

        def fix_step(jj, _):
            out_re, out_im = [], []
            for k in range(2):
                j = 2 * jj + k
                r = pl.ds(pl.multiple_of(j * SUBLANES, SUBLANES), SUBLANES)
                p_re = apow_ref[blk, 0, pl.ds(j, 1), :]
                p_im = apow_ref[blk, 1, pl.ds(j, 1), :]
                out_re.append(s_ref[r, 0:half] + p_re * cin_re - p_im * cin_im)
                out_im.append(s_ref[r, half:2 * half] + p_re * cin_im + p_im * cin_re)
            r2 = pl.ds(pl.multiple_of(jj * 2 * SUBLANES, 2 * SUBLANES), 2 * SUBLANES)
            sb_ref[r2, 0:half] = jnp.concatenate(out_re, axis=0).astype(BF16)
            sb_ref[r2, half:2 * half] = jnp.concatenate(out_im, axis=0).astype(BF16)
            return 0

        lax.fori_loop(0, nj // 2, fix_step, 0, unroll=S5_UNROLL // 2)
        yp_ref[:, blk * LANES:(blk + 1) * LANES] = _dot(sb_ref[...], cmat_ref[blk])

    y = jax.nn.gelu(yp_ref[...] + dskip_ref[...] * u)
    z = y * jax.nn.sigmoid(_dot(y.astype(BF16), wg_ref[...]) + bg_ref[...])
    z = _dot(permt_ref[...], z.astype(BF16)).astype(BF16)
    o_ref[...] = h_ref[...] + _dot(z, wo_ref[...])


def _s5_discretise(lam_re, lam_im, log_dt, b_re, b_im, c_re, c_im, ts):
    groups, states = lam_re.shape
    gpb = S5_BLOCK_GROUPS
    nblk = groups // gpb
    lr = lam_re.astype(F32)
    li = lam_im.astype(F32)
    dt = jnp.exp(log_dt.astype(F32))[:, None]
    mag = jnp.exp(lr * dt)
    ab_re = mag * jnp.cos(li * dt)
    ab_im = mag * jnp.sin(li * dt)
    den = lr * lr + li * li
    z_re = ((ab_re - 1.0) * lr + ab_im * li) / den
    z_im = (ab_im * lr - (ab_re - 1.0) * li) / den
    br = b_re.astype(F32)
    bi = b_im.astype(F32)
    bb_re = z_re[..., None] * br - z_im[..., None] * bi
    bb_im = z_re[..., None] * bi + z_im[..., None] * br
    eye = jnp.eye(gpb, dtype=F32)

    def in_block(bb):
        t = bb.reshape(nblk, gpb, states, C_GROUP)
        return jnp.einsum('ngph,gk->nghkp', t, eye).reshape(nblk, gpb * C_GROUP, gpb * states)

    bmat = jnp.concatenate([in_block(bb_re), in_block(bb_im)], axis=-1).astype(BF16)

    def out_block(cc):
        t = cc.astype(F32).reshape(nblk, gpb, C_GROUP, states)
        return jnp.einsum('nghp,gk->ngpkh', t, eye).reshape(nblk, gpb * states, gpb * C_GROUP)

    cmat = jnp.concatenate([out_block(c_re), -out_block(c_im)], axis=1).astype(BF16)

    def powers(n):
        n = n.astype(F32)[:, None, None]
        m = jnp.exp(n * (lr * dt)[None])
        ang = n * (li * dt)[None]
        pr = (m * jnp.cos(ang)).reshape(-1, nblk, gpb * states).transpose(1, 0, 2)
        pi = (m * jnp.sin(ang)).reshape(-1, nblk, gpb * states).transpose(1, 0, 2)
        return jnp.stack([pr, pi], axis=1)

    nj = ts // SUBLANES
    acoef = jnp.broadcast_to(powers(jnp.ones((1,)))[:, :, 0:1, :], (nblk, 2, SUBLANES, gpb * states))
    apow = powers(jnp.arange(1, nj + 1))
    atile = powers(jnp.full((1,), nj))
    return bmat, cmat, acoef, apow, atile


def _s5_layer(h, gain, w_in, lam_re, lam_im, log_dt, b_re, b_im, c_re, c_im, d_skip, w_gate, b_gate,
              w_out, batch):
    rows, d = h.shape
    seq = rows // batch
    ts = S5_TILE
    nj = ts // SUBLANES
    per_batch = seq // ts
    bmat, cmat, acoef, apow, atile = _s5_discretise(lam_re, lam_im, log_dt, b_re, b_im, c_re, c_im, ts)
    nblk, _, sw = bmat.shape
    p = jnp.arange(ts)
    perm = (p[None, :] == ((p % SUBLANES) * nj + p // SUBLANES)[:, None]).astype(BF16)
    tile = lambda b, i: (b * per_batch + i, 0)
    return pl.pallas_call(
        _s5_kernel,
        grid=(batch, per_batch),
        in_specs=[pl.BlockSpec((ts, d), tile), _whole((1, d)), _whole((d, d)), _whole((ts, ts)), _whole((ts, ts)),
                  _whole(bmat.shape), _whole(acoef.shape), _whole(apow.shape), _whole(atile.shape),
                  _whole(cmat.shape), _whole((1, d)), _whole((d, d)), _whole((1, d)), _whole((d, d))],
        out_specs=pl.BlockSpec((ts, d), tile),
        out_shape=jax.ShapeDtypeStruct((rows, d), F32),
        scratch_shapes=[pltpu.VMEM((nblk, SUBLANES, sw), F32), pltpu.VMEM((ts, sw), F32),
                        pltpu.VMEM((ts, sw), F32), pltpu.VMEM((ts, sw), BF16),
                        pltpu.VMEM((ts, d), F32)],
        compiler_params=_params(("parallel", "arbitrary")),
        name="s5_mix",
    )(h, gain, w_in.astype(BF16), perm, perm.T, bmat, acoef, apow, atile, cmat, d_skip.reshape(1, d),
      w_gate.astype(BF16), b_gate.reshape(1, d), w_out.astype(BF16))


def kernel(x, mem, norm_mix, norm_x, norm_ffn, norm_mem, norm_final, a_w_in, a_conv_w, a_gate_b, a_norm, a_w_out, b_w_in, b_gate_w2, b_gate_b, b_norm, b_w_out, c_w_in, c_lam_re, c_lam_im, c_log_dt, c_b_re, c_b_im, c_c_re, c_c_im, c_d, c_w_gate, c_b_gate, c_w_out, x_w_q, x_w_kv, x_w_o, f_w_gate, f_w_up, f_w_down):
    batch, seq, d = x.shape
    mem_len = mem.shape[1]
    depth = norm_mix.shape[0]
    assert seq % ROW_TILE == 0 and seq % MLSTM_TILE == 0 and seq % GLA_TILE == 0 and seq % S5_TILE == 0
    assert MLSTM_TILE % MIX_CHUNK == 0 and GLA_TILE % GLA_CHUNK == 0 and seq % MLSTM_IN_TILE == 0
    assert f_w_gate.shape[2] % FF_CHUNK == 0 and d % (X_HEADS * LANES) == 0

    kv = _mem_kv(mem.reshape(batch * mem_len, d), norm_mem.reshape(1, d), x_w_kv.astype(BF16), mem_len)

    h = x.reshape(batch * seq, d)
    for i in range(depth):
        kind = i % N_MIXERS
        j = i // N_MIXERS
        gain = norm_mix[i].reshape(1, d)
        if kind == 0:
            h = _mlstm_layer(h, gain, a_w_in[j], a_conv_w[j], a_gate_b[j], a_norm[j], a_w_out[j], batch)
        elif kind == 1:
            h = _gla_layer(h, gain, b_w_in[j], b_gate_w2[j], b_gate_b[j], b_norm[j], b_w_out[j], batch)
        else:
            h = _s5_layer(h, gain, c_w_in[j], c_lam_re[j], c_lam_im[j], c_log_dt[j], c_b_re[j], c_b_im[j],
                          c_c_re[j], c_c_im[j], c_d[j], c_w_gate[j], c_b_gate[j], c_w_out[j], batch)
        h = _xattn(h, norm_x[i].reshape(1, d), x_w_q[i].astype(BF16), kv, i, x_w_o[i].astype(BF16),
                   batch, mem_len)
        h = _swiglu(h, norm_ffn[i].reshape(1, d), f_w_gate[i].astype(BF16), f_w_up[i].astype(BF16),
                    f_w_down[i].astype(BF16), norm_final.reshape(1, d), final_norm=(i == depth - 1))
    return h.reshape(batch, seq, d)
```

```python
import functools
import math

import jax
import jax.numpy as jnp
from jax import lax
from jax.experimental import pallas as pl
from jax.experimental.pallas import tpu as pltpu

F32 = jnp.float32
BF16 = jnp.bfloat16
EPS = 1e-6

N_MIXERS = 3
A_HEADS = 4
A_CONV = 4
B_HEADS = 4
B_RANK = 16
B_TAU = 16.0
C_GROUP = 16
C_STATE = 64
X_HEADS = 4

LANES = 128
SUBLANES = 8
VMEM_LIMIT_BYTES = 56 * 1024 * 1024

ROW_TILE = 1024
FF_TILE = 512
SUB_TILE = 256
MLSTM_IN_TILE = 1024
MLSTM_TILE = 1024
GLA_TILE = 1024
MIX_CHUNK = 512
GLA_CHUNK = 128
S5_TILE = 512
S5_UNROLL = S5_TILE // SUBLANES
FF_CHUNK = 256
S5_BLOCK_GROUPS = LANES // C_GROUP


def _params(semantics):
    return pltpu.CompilerParams(dimension_semantics=semantics, vmem_limit_bytes=VMEM_LIMIT_BYTES)


def _whole(shape):
    zeros = (0,) * len(shape)
    return pl.BlockSpec(shape, lambda *_: zeros)


def _rms(x, g):
    return x * lax.rsqrt(jnp.mean(x * x, axis=-1, keepdims=True) + EPS) * g


def _dot(a, b):
    return jnp.dot(a, b, preferred_element_type=F32)


def _dot_nt(a, b):
    return lax.dot_general(a, b, (((1,), (1,)), ((), ())), preferred_element_type=F32)


def _dot_tn(a, b):
    return lax.dot_general(a, b, (((0,), (0,)), ((), ())), preferred_element_type=F32)


def _split3(x):
    hi = x.astype(BF16)
    r = x - hi.astype(F32)
    mid = r.astype(BF16)
    lo = (r - mid.astype(F32)).astype(BF16)
    return hi, mid, lo


def _dot_exact_lhs01(m01, x):
    hi, mid, lo = _split3(x)
    return _dot(m01, hi) + _dot(m01, mid) + _dot(m01, lo)


def _head_rms(o, g):
    return o * lax.rsqrt(jnp.mean(o * o, axis=-1, keepdims=True) + EPS) * g


def _mem_kv_kernel(x_ref, g_ref, w_ref, o_ref):
    xn = _rms(x_ref[...], g_ref[...]).astype(BF16)
    o_ref[...] = _dot(xn, w_ref[0]).astype(BF16)


def _mem_kv(x, g, w, tm):
    rows, d = x.shape
    depth, _, n = w.shape
    return pl.pallas_call(
        _mem_kv_kernel,
        grid=(depth, rows // tm),
        in_specs=[pl.BlockSpec((tm, d), lambda l, i: (i, 0)), _whole((1, d)),
                  pl.BlockSpec((1, d, n), lambda l, i: (l, 0, 0))],
        out_specs=pl.BlockSpec((tm, n), lambda l, i: (i, l)),
        out_shape=jax.ShapeDtypeStruct((rows, depth * n), BF16),
        compiler_params=_params(("parallel", "parallel")),
        name="mem_kv_proj",
    )(x, g, w)


def _xattn_kernel(h_ref, g_ref, wq_ref, kv_ref, wo_ref, o_ref):
    d = h_ref.shape[1]
    dh = d // X_HEADS
    x = h_ref[...]
    xn = _rms(x, g_ref[...]).astype(BF16)
    q = _dot(xn, wq_ref[...]).astype(BF16)
    outs = []
    for hd in range(X_HEADS):
        qh = q[:, hd * dh:(hd + 1) * dh]
        kh = kv_ref[:, hd * dh:(hd + 1) * dh]
        vh = kv_ref[:, d + hd * dh:d + (hd + 1) * dh]
        s = _dot_nt(qh, kh) * (dh ** -0.5)
        e = jnp.exp(s - jnp.max(s, axis=-1, keepdims=True))
        p = e / jnp.sum(e, axis=-1, keepdims=True)
        outs.append(_dot(p.astype(BF16), vh))
    o = jnp.concatenate(outs, axis=-1).astype(BF16)
    o_ref[...] = x + _dot(o, wo_ref[...])


def _xattn(h, g, wq, kv, layer, wo, batch, mem_len):
    rows, d = h.shape
    tm = ROW_TILE
    per_batch = rows // batch // tm
    return pl.pallas_call(
        _xattn_kernel,
        grid=(batch, per_batch),
        in_specs=[
            pl.BlockSpec((tm, d), lambda b, i: (b * per_batch + i, 0)),
            _whole((1, d)),
            _whole((d, d)),
            pl.BlockSpec((mem_len, 2 * d), lambda b, i: (b, layer)),
            _whole((d, d)),
        ],
        out_specs=pl.BlockSpec((tm, d), lambda b, i: (b * per_batch + i, 0)),
        out_shape=jax.ShapeDtypeStruct((rows, d), F32),
        compiler_params=_params(("parallel", "parallel")),
        name="xattn",
    )(h, g, wq, kv, wo)


def _swiglu_kernel(h_ref, g_ref, wg_ref, wu_ref, wd_ref, gf_ref, o_ref, *, final_norm):
    x = h_ref[...]
    xn = _rms(x, g_ref[...]).astype(BF16)
    o_ref[...] = x
    for c in range(wg_ref.shape[1] // FF_CHUNK):
        cols = slice(c * FF_CHUNK, (c + 1) * FF_CHUNK)
        gate = _dot(xn, wg_ref[:, cols])
        up = _dot(xn, wu_ref[:, cols])
        act = (jax.nn.silu(gate) * up).astype(BF16)
        o_ref[...] += _dot(act, wd_ref[cols, :])
    if final_norm:
        o_ref[...] = _rms(o_ref[...], gf_ref[...])


def _swiglu(h, g, wg, wu, wd, gf, final_norm):
    rows, d = h.shape
    tm = FF_TILE
    return pl.pallas_call(
        functools.partial(_swiglu_kernel, final_norm=final_norm),
        grid=(rows // tm,),
        in_specs=[pl.BlockSpec((tm, d), lambda i: (i, 0)), _whole((1, d)), _whole(wg.shape),
                  _whole(wu.shape), _whole(wd.shape), _whole((1, d))],
        out_specs=pl.BlockSpec((tm, d), lambda i: (i, 0)),
        out_shape=jax.ShapeDtypeStruct((rows, d), F32),
        compiler_params=_params(("parallel",)),
        name="swiglu",
    )(h, g, wg, wu, wd, gf)


def _mlstm_project(h_ref, g_ref, wqk_ref, wv_ref, wo_ref, wgt_ref, cw_ref, gb_ref,
                   q_ref, k_ref, v_ref, og_ref, gr_ref, xp_ref):
    tm, d = h_ref.shape
    nqk = wqk_ref.shape[1]
    dk = nqk // (2 * A_HEADS)
    cw = cw_ref[...]
    xn_all = _rms(h_ref[...], g_ref[...]).astype(BF16)

    gr = _dot_nt(wgt_ref[...], xn_all) + gb_ref[...]
    is_input_gate = lax.broadcasted_iota(jnp.int32, gr.shape, 0) < A_HEADS
    gr = jnp.where(is_input_gate, gr, jax.nn.log_sigmoid(gr))
    pos = lax.broadcasted_iota(jnp.int32, gr.shape, 1) % MIX_CHUNK

    def chunk_scan(x, op):
        k = 1
        while k < MIX_CHUNK:
            x = jnp.where(pos >= k, op(x, pltpu.roll(x, k, axis=1)), x)
            k *= 2
        return x

    b = chunk_scan(gr, jnp.add)
    beta = pltpu.roll(gr, A_HEADS, axis=0) - b
    ng = gr.shape[0]
    gr_ref[0:ng, :] = beta
    gr_ref[ng:2 * ng, :] = b
    gr_ref[2 * ng:3 * ng, :] = chunk_scan(beta, jnp.maximum)

    for r0 in range(0, tm, SUB_TILE):
        rows = slice(r0, r0 + SUB_TILE)
        xn = xn_all[rows, :]
        qk = _dot(xn, wqk_ref[...])
        xp_ref[SUBLANES + r0:SUBLANES + r0 + SUB_TILE, :] = qk
        y = cw[A_CONV - 1:A_CONV, :] * qk
        for j in range(A_CONV - 1):
            y = y + cw[j:j + 1, :] * xp_ref[pl.ds(r0 + SUBLANES - (A_CONV - 1) + j, SUB_TILE), :]
        y = jax.nn.silu(y)
        q_ref[rows, :] = (y[:, :nqk // 2] * (dk ** -0.5)).astype(BF16)
        k_ref[rows, :] = y[:, nqk // 2:].astype(BF16)
        v_ref[rows, :] = _dot(xn, wv_ref[...]).astype(BF16)
        og_ref[rows, :] = jax.nn.sigmoid(_dot(xn, wo_ref[...])).astype(BF16)
    xp_ref[0:SUBLANES, :] = xp_ref[tm:tm + SUBLANES, :]


def _mlstm_recur(q_ref, k_ref, v_ref, og_ref, gr_ref, ng_ref, wout_ref, h_ref, o_ref,
                 cn_ref, m_ref, y_ref):
    ts = q_ref.shape[0]
    dk = q_ref.shape[1] // A_HEADS
    dv = v_ref.shape[1] // A_HEADS
    L = MIX_CHUNK
    H = A_HEADS
    causal = lax.broadcasted_iota(jnp.int32, (L, L), 1) <= lax.broadcasted_iota(jnp.int32, (L, L), 0)
    ones = jnp.ones((L, LANES), BF16)

    for c in range(ts // L):
        rows = slice(c * L, (c + 1) * L)
        beta = gr_ref[0:2 * H, rows]
        b_rows = gr_ref[2 * H:4 * H, rows]
        m_prev = m_ref[:, 0:1]
        big_m = jnp.maximum(m_prev, gr_ref[4 * H:6 * H, rows])
        m_last = big_m[:, L - 1:L]
        decay_all = jnp.exp(m_prev - m_last)
        m_ref[...] = jnp.broadcast_to(b_rows[:, L - 1:L] + m_last, m_ref.shape)
        m_cols = big_m.T
        w_inter_cols = jnp.exp(m_prev - big_m).T
        exp_neg_m_cols = jnp.exp(-(b_rows + big_m)).T
        wa_cols = jnp.exp(beta - m_last).T
        for hd in range(A_HEADS):
            j = H + hd
            qh = q_ref[rows, hd * dk:(hd + 1) * dk]
            kh = k_ref[rows, hd * dk:(hd + 1) * dk]
            v1 = jnp.concatenate([v_ref[rows, hd * dv:(hd + 1) * dv], ones], axis=1)
            cn_prev = cn_ref[hd]
            w_intra = jnp.exp(jnp.where(causal, beta[j:j + 1, :] - m_cols[:, j:j + 1], -jnp.inf))
            w_inter = w_inter_cols[:, j:j + 1]
            s = (_dot_nt(qh, kh) * w_intra).astype(BF16)
            both = _dot(s, v1) + w_inter * _dot(qh, cn_prev.astype(BF16))
            inv = 1.0 / jnp.maximum(jnp.abs(both[:, dv:]), exp_neg_m_cols[:, j:j + 1])
            hh = both[:, :dv] * jnp.concatenate([inv] * (dv // LANES), axis=1)
            kw = (kh.astype(F32) * wa_cols[:, j:j + 1]).astype(BF16)
            cn_ref[hd] = decay_all[j:j + 1, :] * cn_prev + _dot_tn(kw, v1)
            cols = slice(hd * dv, (hd + 1) * dv)
            yh = _head_rms(hh, ng_ref[:, cols]) * og_ref[rows, cols].astype(F32)
            y_ref[rows, cols] = yh.astype(BF16)

    o_ref[...] = h_ref[...] + _dot(y_ref[...], wout_ref[...])


def _mlstm_in_kernel(h_ref, g_ref, wqk_ref, wv_ref, wo_ref, wgt_ref, cw_ref, gb_ref,
                     q_ref, k_ref, v_ref, og_ref, gr_ref, xp_ref, *, tiles_per_seq):
    @pl.when(pl.program_id(0) % tiles_per_seq == 0)
    def _():
        xp_ref[0:SUBLANES, :] = jnp.zeros((SUBLANES, xp_ref.shape[1]), F32)

    _mlstm_project(h_ref, g_ref, wqk_ref, wv_ref, wo_ref, wgt_ref, cw_ref, gb_ref,
                   q_ref, k_ref, v_ref, og_ref, gr_ref, xp_ref)


def _mlstm_scan_kernel(q_ref, k_ref, v_ref, og_ref, gr_ref, ng_ref, wout_ref, h_ref, o_ref,
                       cn_ref, m_ref, y_ref):
    @pl.when(pl.program_id(1) == 0)
    def _():
        cn_ref[...] = jnp.zeros(cn_ref.shape, F32)
        m_ref[...] = jnp.zeros(m_ref.shape, F32)

    _mlstm_recur(q_ref, k_ref, v_ref, og_ref, gr_ref, ng_ref, wout_ref, h_ref, o_ref,
                 cn_ref, m_ref, y_ref)


def _mlstm_layer(h, gain, w_in, conv_w, gate_b, norm_g, w_out, batch):
    rows, d = h.shape
    seq = rows // batch
    nqk = conv_w.shape[1]
    dk = nqk // (2 * A_HEADS)
    dv = d // A_HEADS
    nv = A_HEADS * dv
    ng = 2 * A_HEADS
    wqk = w_in[:, :nqk].astype(BF16)
    wv = w_in[:, nqk:nqk + nv].astype(BF16)
    wo = w_in[:, nqk + nv:nqk + 2 * nv].astype(BF16)
    wgt = w_in[:, nqk + 2 * nv:].T.astype(BF16)
    tm = MLSTM_IN_TILE
    q, k, v, og, gr = pl.pallas_call(
        functools.partial(_mlstm_in_kernel, tiles_per_seq=seq // tm),
        grid=(rows // tm,),
        in_specs=[pl.BlockSpec((tm, d), lambda i: (i, 0)), _whole((1, d)), _whole((d, nqk)),
                  _whole((d, nv)), _whole((d, nv)), _whole((ng, d)), _whole((A_CONV, nqk)),
                  _whole((ng, 1))],
        out_specs=[pl.BlockSpec((tm, nqk // 2), lambda i: (i, 0)),
                   pl.BlockSpec((tm, nqk // 2), lambda i: (i, 0)),
                   pl.BlockSpec((tm, nv), lambda i: (i, 0)),
                   pl.BlockSpec((tm, nv), lambda i: (i, 0)),
                   pl.BlockSpec((3 * ng, tm), lambda i: (0, i))],
        out_shape=[jax.ShapeDtypeStruct((rows, nqk // 2), BF16),
                   jax.ShapeDtypeStruct((rows, nqk // 2), BF16),
                   jax.ShapeDtypeStruct((rows, nv), BF16),
                   jax.ShapeDtypeStruct((rows, nv), BF16),
                   jax.ShapeDtypeStruct((3 * ng, rows), F32)],
        scratch_shapes=[pltpu.VMEM((tm + SUBLANES, nqk), F32)],
        compiler_params=_params(("arbitrary",)),
        name="mlstm_in",
    )(h, gain, wqk, wv, wo, wgt, conv_w, gate_b.reshape(ng, 1))

    ts = MLSTM_TILE
    per_batch = seq // ts
    tile = lambda b, i: (b * per_batch + i, 0)
    return pl.pallas_call(
        _mlstm_scan_kernel,
        grid=(batch, per_batch),
        in_specs=[pl.BlockSpec((ts, nqk // 2), tile), pl.BlockSpec((ts, nqk // 2), tile),
                  pl.BlockSpec((ts, nv), tile), pl.BlockSpec((ts, nv), tile),
                  pl.BlockSpec((3 * ng, ts), lambda b, i: (0, b * per_batch + i)),
                  _whole((1, nv)), _whole((nv, d)), pl.BlockSpec((ts, d), tile)],
        out_specs=pl.BlockSpec((ts, d), tile),
        out_shape=jax.ShapeDtypeStruct((rows, d), F32),
        scratch_shapes=[pltpu.VMEM((A_HEADS, dk, dv + LANES), F32),
                        pltpu.VMEM((ng, LANES), F32),
                        pltpu.VMEM((ts, nv), BF16)],
        compiler_params=_params(("parallel", "arbitrary")),
        name="mlstm_scan",
    )(q, k, v, og, gr, norm_g.reshape(1, nv), w_out.astype(BF16), h)


def _gla_in_kernel(h_ref, g_ref, wq_ref, wk_ref, wv_ref, wr_ref, wg1_ref, wg2_ref, gb_ref,
                   q_ref, k_ref, v_ref, rg_ref, la_ref):
    dk = wq_ref.shape[1] // B_HEADS
    xn = _rms(h_ref[...], g_ref[...]).astype(BF16)
    q_ref[...] = (_dot(xn, wq_ref[...]) * (dk ** -0.5)).astype(BF16)
    k_ref[...] = _dot(xn, wk_ref[...]).astype(BF16)
    v_ref[...] = _dot(xn, wv_ref[...]).astype(BF16)
    rg_ref[...] = jax.nn.silu(_dot(xn, wr_ref[...])).astype(BF16)
    g_low = _dot(xn, wg1_ref[...]).astype(BF16)
    la_ref[...] = jax.nn.log_sigmoid(_dot(g_low, wg2_ref[...]) + gb_ref[...]) / B_TAU


def _bcast_sublane(x3, idx):
    return jnp.broadcast_to(x3[:, idx:idx + 1, :], x3.shape)


def _boundary_rows(b, half):
    L, w = b.shape
    if half >= SUBLANES:
        nb = L // (2 * half)
        b4 = b.reshape(nb, 2 * half // SUBLANES, SUBLANES, w)
        last = b4[:, half // SUBLANES - 1, SUBLANES - 1:SUBLANES, :]
        return jnp.broadcast_to(last[:, None, :, :], b4.shape).reshape(L, w)
    b3 = b.reshape(L // SUBLANES, SUBLANES, w)
    sub = lax.broadcasted_iota(jnp.int32, b3.shape, 1)
    out = None
    for blk in range(SUBLANES // (2 * half)):
        cand = _bcast_sublane(b3, blk * 2 * half + half - 1)
        out = cand if out is None else jnp.where(sub >= blk * 2 * half, cand, out)
    return out.reshape(L, w)


def _gla_scan_kernel(q_ref, k_ref, v_ref, rg_ref, la_ref, ng_ref, wout_ref, h_ref, o_ref,
                     st_ref, y_ref):
    ts = q_ref.shape[0]
    dk = q_ref.shape[1] // B_HEADS
    dv = v_ref.shape[1] // B_HEADS
    L = GLA_CHUNK

    @pl.when(pl.program_id(1) == 0)
    def _():
        st_ref[...] = jnp.zeros(st_ref.shape, F32)

    row = lax.broadcasted_iota(jnp.int32, (L, L), 0)
    col = lax.broadcasted_iota(jnp.int32, (L, L), 1)
    tri = (col <= row).astype(BF16)
    eye = (col == row).astype(F32)
    eye_k = (lax.broadcasted_iota(jnp.int32, (dk, dk), 0) == lax.broadcasted_iota(jnp.int32, (dk, dk), 1))

    levels = []
    half = L // 2
    while half >= 1:
        pairs = ((row // (2 * half)) == (col // (2 * half))) & ((row & half) != 0) & ((col & half) == 0)
        levels.append((half, pairs.astype(F32)))
        half //= 2

    for c in range(ts // L):
        rows = slice(c * L, (c + 1) * L)
        b_all = _dot_exact_lhs01(tri, la_ref[rows, :])
        for hd in range(B_HEADS):
            kc = slice(hd * dk, (hd + 1) * dk)
            vc = slice(hd * dv, (hd + 1) * dv)
            b = b_all[:, kc]
            qh = q_ref[rows, kc].astype(F32)
            kh = k_ref[rows, kc].astype(F32)
            vh = v_ref[rows, vc]
            s_mat = eye * _dot_nt(qh.astype(BF16), kh.astype(BF16))
            for half, pairs in levels:
                e = jnp.exp(-jnp.abs(b - _boundary_rows(b, half)))
                s_mat = s_mat + pairs * _dot_nt((qh * e).astype(BF16), (kh * e).astype(BF16))
            state = st_ref[hd]
            o = _dot(s_mat.astype(BF16), vh) + _dot((qh * jnp.exp(b)).astype(BF16), state.astype(BF16))
            b_last = b[L - 1:L, :]
            k_dec = kh * jnp.exp(b_last - b)
            decay_col = jnp.sum(jnp.where(eye_k, jnp.exp(b_last), 0.0), axis=1, keepdims=True)
            st_ref[hd] = decay_col * state + _dot_tn(k_dec.astype(BF16), vh)
            yh = _head_rms(o, ng_ref[:, vc]) * rg_ref[rows, vc].astype(F32)
            y_ref[rows, vc] = yh.astype(BF16)

    o_ref[...] = h_ref[...] + _dot(y_ref[...], wout_ref[...])


def _gla_layer(h, gain, w_in, gate_w2, gate_b, norm_g, w_out, batch):
    rows, d = h.shape
    seq = rows // batch
    nk = gate_w2.shape[1]
    dk = nk // B_HEADS
    dv = d // B_HEADS
    nv = B_HEADS * dv
    wq = w_in[:, :nk].astype(BF16)
    wk = w_in[:, nk:2 * nk].astype(BF16)
    wv = w_in[:, 2 * nk:2 * nk + nv].astype(BF16)
    wr = w_in[:, 2 * nk + nv:2 * nk + 2 * nv].astype(BF16)
    wg1 = jnp.pad(w_in[:, 2 * nk + 2 * nv:], ((0, 0), (0, LANES - B_RANK))).astype(BF16)
    wg2 = jnp.pad(gate_w2, ((0, LANES - B_RANK), (0, 0))).astype(BF16)
    tm = ROW_TILE
    row_tile = lambda i: (i, 0)
    q, k, v, rg, la = pl.pallas_call(
        _gla_in_kernel,
        grid=(rows // tm,),
        in_specs=[pl.BlockSpec((tm, d), row_tile), _whole((1, d)), _whole((d, nk)), _whole((d, nk)),
                  _whole((d, nv)), _whole((d, nv)), _whole((d, LANES)), _whole((LANES, nk)),
                  _whole((1, nk))],
        out_specs=[pl.BlockSpec((tm, nk), row_tile), pl.BlockSpec((tm, nk), row_tile),
                   pl.BlockSpec((tm, nv), row_tile), pl.BlockSpec((tm, nv), row_tile),
                   pl.BlockSpec((tm, nk), row_tile)],
        out_shape=[jax.ShapeDtypeStruct((rows, nk), BF16), jax.ShapeDtypeStruct((rows, nk), BF16),
                   jax.ShapeDtypeStruct((rows, nv), BF16), jax.ShapeDtypeStruct((rows, nv), BF16),
                   jax.ShapeDtypeStruct((rows, nk), F32)],
        compiler_params=_params(("parallel",)),
        name="gla_in",
    )(h, gain, wq, wk, wv, wr, wg1, wg2, gate_b.reshape(1, nk))

    ts = GLA_TILE
    per_batch = seq // ts
    tile = lambda b, i: (b * per_batch + i, 0)
    return pl.pallas_call(
        _gla_scan_kernel,
        grid=(batch, per_batch),
        in_specs=[pl.BlockSpec((ts, nk), tile), pl.BlockSpec((ts, nk), tile),
                  pl.BlockSpec((ts, nv), tile), pl.BlockSpec((ts, nv), tile),
                  pl.BlockSpec((ts, nk), tile), _whole((1, nv)), _whole((nv, d)),
                  pl.BlockSpec((ts, d), tile)],
        out_specs=pl.BlockSpec((ts, d), tile),
        out_shape=jax.ShapeDtypeStruct((rows, d), F32),
        scratch_shapes=[pltpu.VMEM((B_HEADS, dk, dv), F32), pltpu.VMEM((ts, nv), BF16)],
        compiler_params=_params(("parallel", "arbitrary")),
        name="gla_scan",
    )(q, k, v, rg, la, norm_g.reshape(1, nv), w_out.astype(BF16), h)


def _s5_kernel(h_ref, g_ref, win_ref, perm_ref, permt_ref, bmat_ref, acoef_ref, apow_ref, atile_ref,
               cmat_ref, dskip_ref, wg_ref, bg_ref, wo_ref, o_ref, carry_ref, bu_ref, s_ref, sb_ref, yp_ref):
    ts, d = h_ref.shape
    nblk = bmat_ref.shape[0]
    half = bmat_ref.shape[2] // 2
    nj = ts // SUBLANES

    @pl.when(pl.program_id(1) == 0)
    def _():
        carry_ref[...] = jnp.zeros(carry_ref.shape, F32)

    xn = _rms(h_ref[...], g_ref[...]).astype(BF16)
    xp = _dot(perm_ref[...], xn).astype(BF16)
    u = _dot(xp, win_ref[...])
    ub = u.astype(BF16)

    for blk in range(nblk):
        bu_ref[...] = _dot(ub[:, blk * LANES:(blk + 1) * LANES], bmat_ref[blk])
        a_re = acoef_ref[blk, 0]
        a_im = acoef_ref[blk, 1]

        def local_step(j, x):
            x_re, x_im = x
            r = pl.ds(pl.multiple_of(j * SUBLANES, SUBLANES), SUBLANES)
            n_re = a_re * x_re - a_im * x_im + bu_ref[r, 0:half]
            n_im = a_re * x_im + a_im * x_re + bu_ref[r, half:2 * half]
            s_ref[r, 0:half] = n_re
            s_ref[r, half:2 * half] = n_im
            return n_re, n_im

        zero = jnp.zeros((SUBLANES, half), F32)
        end_re, end_im = lax.fori_loop(0, nj, local_step, (zero, zero), unroll=S5_UNROLL)

        t_re = atile_ref[blk, 0]
        t_im = atile_ref[blk, 1]
        c_re = carry_ref[blk, 0:1, 0:half]
        c_im = carry_ref[blk, 0:1, half:2 * half]
        cin_re, cin_im = [], []
        for r in range(SUBLANES):
            cin_re.append(c_re)
            cin_im.append(c_im)
            e_re = end_re[r:r + 1, :] + t_re * c_re - t_im * c_im
            e_im = end_im[r:r + 1, :] + t_re * c_im + t_im * c_re
            c_re, c_im = e_re, e_im
        carry_ref[blk, :, 0:half] = jnp.broadcast_to(c_re, (SUBLANES, half))
        carry_ref[blk, :, half:2 * half] = jnp.broadcast_to(c_im, (SUBLANES, half))
        cin_re = jnp.concatenate(cin_re, axis=0)
        cin_im = jnp.concatenate(cin_im, axis=0)

        def fix_step(jj, _):
            out_re, out_im = [], []
            for k in range(2):
                j = 2 * jj + k
                r = pl.ds(pl.multiple_of(j * SUBLANES, SUBLANES), SUBLANES)
                p_re = apow_ref[blk, 0, pl.ds(j, 1), :]
                p_im = apow_ref[blk, 1, pl.ds(j, 1), :]
                out_re.append(s_ref[r, 0:half] + p_re * cin_re - p_im * cin_im)
                out_im.append(s_ref[r, half:2 * half] + p_re * cin_im + p_im * cin_re)
            r2 = pl.ds(pl.multiple_of(jj * 2 * SUBLANES, 2 * SUBLANES), 2 * SUBLANES)
            sb_ref[r2, 0:half] = jnp.concatenate(out_re, axis=0).astype(BF16)
            sb_ref[r2, half:2 * half] = jnp.concatenate(out_im, axis=0).astype(BF16)
            return 0

        lax.fori_loop(0, nj // 2, fix_step, 0, unroll=S5_UNROLL // 2)
        yp_ref[:, blk * LANES:(blk + 1) * LANES] = _dot(sb_ref[...], cmat_ref[blk])

    y = jax.nn.gelu(yp_ref[...] + dskip_ref[...] * u)
    z = y * jax.nn.sigmoid(_dot(y.astype(BF16), wg_ref[...]) + bg_ref[...])
    z = _dot(permt_ref[...], z.astype(BF16)).astype(BF16)
    o_ref[...] = h_ref[...] + _dot(z, wo_ref[...])


def _s5_discretise(lam_re, lam_im, log_dt, b_re, b_im, c_re, c_im, ts):
    groups, states = lam_re.shape
    gpb = S5_BLOCK_GROUPS
    nblk = groups // gpb
    lr = lam_re.astype(F32)
    li = lam_im.astype(F32)
    dt = jnp.exp(log_dt.astype(F32))[:, None]
    mag = jnp.exp(lr * dt)
    ab_re = mag * jnp.cos(li * dt)
    ab_im = mag * jnp.sin(li * dt)
    den = lr * lr + li * li
    z_re = ((ab_re - 1.0) * lr + ab_im * li) / den
    z_im = (ab_im * lr - (ab_re - 1.0) * li) / den
    br = b_re.astype(F32)
    bi = b_im.astype(F32)
    bb_re = z_re[..., None] * br - z_im[..., None] * bi
    bb_im = z_re[..., None] * bi + z_im[..., None] * br
    eye = jnp.eye(gpb, dtype=F32)

    def in_block(bb):
        t = bb.reshape(nblk, gpb, states, C_GROUP)
        return jnp.einsum('ngph,gk->nghkp', t, eye).reshape(nblk, gpb * C_GROUP, gpb * states)

    bmat = jnp.concatenate([in_block(bb_re), in_block(bb_im)], axis=-1).astype(BF16)

    def out_block(cc):
        t = cc.astype(F32).reshape(nblk, gpb, C_GROUP, states)
        return jnp.einsum('nghp,gk->ngpkh', t, eye).reshape(nblk, gpb * states, gpb * C_GROUP)

    cmat = jnp.concatenate([out_block(c_re), -out_block(c_im)], axis=1).astype(BF16)

    def powers(n):
        n = n.astype(F32)[:, None, None]
        m = jnp.exp(n * (lr * dt)[None])
        ang = n * (li * dt)[None]
        pr = (m * jnp.cos(ang)).reshape(-1, nblk, gpb * states).transpose(1, 0, 2)
        pi = (m * jnp.sin(ang)).reshape(-1, nblk, gpb * states).transpose(1, 0, 2)
        return jnp.stack([pr, pi], axis=1)

    nj = ts // SUBLANES
    acoef = jnp.broadcast_to(powers(jnp.ones((1,)))[:, :, 0:1, :], (nblk, 2, SUBLANES, gpb * states))
    apow = powers(jnp.arange(1, nj + 1))
    atile = powers(jnp.full((1,), nj))
    return bmat, cmat, acoef, apow, atile


def _s5_layer(h, gain, w_in, lam_re, lam_im, log_dt, b_re, b_im, c_re, c_im, d_skip, w_gate, b_gate,
              w_out, batch):
    rows, d = h.shape
    seq = rows // batch
    ts = S5_TILE
    nj = ts // SUBLANES
    per_batch = seq // ts
    bmat, cmat, acoef, apow, atile = _s5_discretise(lam_re, lam_im, log_dt, b_re, b_im, c_re, c_im, ts)
    nblk, _, sw = bmat.shape
    p = jnp.arange(ts)
    perm = (p[None, :] == ((p % SUBLANES) * nj + p // SUBLANES)[:, None]).astype(BF16)
    tile = lambda b, i: (b * per_batch + i, 0)
    return pl.pallas_call(
        _s5_kernel,
        grid=(batch, per_batch),
        in_specs=[pl.BlockSpec((ts, d), tile), _whole((1, d)), _whole((d, d)), _whole((ts, ts)), _whole((ts, ts)),
                  _whole(bmat.shape), _whole(acoef.shape), _whole(apow.shape), _whole(atile.shape),
                  _whole(cmat.shape), _whole((1, d)), _whole((d, d)), _whole((1, d)), _whole((d, d))],
        out_specs=pl.BlockSpec((ts, d), tile),
        out_shape=jax.ShapeDtypeStruct((rows, d), F32),
        scratch_shapes=[pltpu.VMEM((nblk, SUBLANES, sw), F32), pltpu.VMEM((ts, sw), F32),
                        pltpu.VMEM((ts, sw), F32), pltpu.VMEM((ts, sw), BF16),
                        pltpu.VMEM((ts, d), F32)],
        compiler_params=_params(("parallel", "arbitrary")),
        name="s5_mix",
    )(h, gain, w_in.astype(BF16), perm, perm.T, bmat, acoef, apow, atile, cmat, d_skip.reshape(1, d),
      w_gate.astype(BF16), b_gate.reshape(1, d), w_out.astype(BF16))


def kernel(x, mem, norm_mix, norm_x, norm_ffn, norm_mem, norm_final, a_w_in, a_conv_w, a_gate_b, a_norm, a_w_out, b_w_in, b_gate_w2, b_gate_b, b_norm, b_w_out, c_w_in, c_lam_re, c_lam_im, c_log_dt, c_b_re, c_b_im, c_c_re, c_c_im, c_d, c_w_gate, c_b_gate, c_w_out, x_w_q, x_w_kv, x_w_o, f_w_gate, f_w_up, f_w_down):
    batch, seq, d = x.shape
    mem_len = mem.shape[1]
    depth = norm_mix.shape[0]
    assert seq % ROW_TILE == 0 and seq % MLSTM_TILE == 0 and seq % GLA_TILE == 0 and seq % S5_TILE == 0
    assert MLSTM_TILE % MIX_CHUNK == 0 and GLA_TILE % GLA_CHUNK == 0 and seq % MLSTM_IN_TILE == 0
    assert f_w_gate.shape[2] % FF_CHUNK == 0 and d % (X_HEADS * LANES) == 0 and (batch * seq) % FF_TILE == 0

    kv = _mem_kv(mem.reshape(batch * mem_len, d), norm_mem.reshape(1, d), x_w_kv.astype(BF16), mem_len)

    h = x.reshape(batch * seq, d)
    for i in range(depth):
        kind = i % N_MIXERS
        j = i // N_MIXERS
        gain = norm_mix[i].reshape(1, d)
        if kind == 0:
            h = _mlstm_layer(h, gain, a_w_in[j], a_conv_w[j], a_gate_b[j], a_norm[j], a_w_out[j], batch)
        elif kind == 1:
            h = _gla_layer(h, gain, b_w_in[j], b_gate_w2[j], b_gate_b[j], b_norm[j], b_w_out[j], batch)
        else:
            h = _s5_layer(h, gain, c_w_in[j], c_lam_re[j], c_lam_im[j], c_log_dt[j], c_b_re[j], c_b_im[j],
                          c_c_re[j], c_c_im[j], c_d[j], c_w_gate[j], c_b_gate[j], c_w_out[j], batch)
        h = _xattn(h, norm_x[i].reshape(1, d), x_w_q[i].astype(BF16), kv, i, x_w_o[i].astype(BF16),
                   batch, mem_len)
        h = _swiglu(h, norm_ffn[i].reshape(1, d), f_w_gate[i].astype(BF16), f_w_up[i].astype(BF16),
                    f_w_down[i].astype(BF16), norm_final.reshape(1, d), final_norm=(i == depth - 1))
    return h.reshape(batch, seq, d)
```

```python
import functools
import math

import jax
import jax.numpy as jnp
from jax import lax
from jax.experimental import pallas as pl
from jax.experimental.pallas import tpu as pltpu

F32 = jnp.float32
BF16 = jnp.bfloat16
EPS = 1e-6

N_MIXERS = 3
A_HEADS = 4
A_CONV = 4
B_HEADS = 4
B_RANK = 16
B_TAU = 16.0
C_GROUP = 16
C_STATE = 64
X_HEADS = 4

LANES = 128
SUBLANES = 8
VMEM_LIMIT_BYTES = 56 * 1024 * 1024

ROW_TILE = 1024
FF_TILE = 512
SUB_TILE = 256
MLSTM_IN_TILE = 1024
MLSTM_TILE = 1024
GLA_TILE = 1024
MIX_CHUNK = 512
GLA_CHUNK = 128
S5_TILE = 512
S5_UNROLL = S5_TILE // SUBLANES
FF_CHUNK = 256
S5_BLOCK_GROUPS = 2 * LANES // C_GROUP


def _params(semantics):
    return pltpu.CompilerParams(dimension_semantics=semantics, vmem_limit_bytes=VMEM_LIMIT_BYTES)


def _whole(shape):
    zeros = (0,) * len(shape)
    return pl.BlockSpec(shape, lambda *_: zeros)


def _rms(x, g):
    return x * lax.rsqrt(jnp.mean(x * x, axis=-1, keepdims=True) + EPS) * g


def _dot(a, b):
    return jnp.dot(a, b, preferred_element_type=F32)


def _dot_nt(a, b):
    return lax.dot_general(a, b, (((1,), (1,)), ((), ())), preferred_element_type=F32)


def _dot_tn(a, b):
    return lax.dot_general(a, b, (((0,), (0,)), ((), ())), preferred_element_type=F32)


def _split3(x):
    hi = x.astype(BF16)
    r = x - hi.astype(F32)
    mid = r.astype(BF16)
    lo = (r - mid.astype(F32)).astype(BF16)
    return hi, mid, lo


def _dot_exact_lhs01(m01, x):
    hi, mid, lo = _split3(x)
    return _dot(m01, hi) + _dot(m01, mid) + _dot(m01, lo)


def _head_rms(o, g):
    return o * lax.rsqrt(jnp.mean(o * o, axis=-1, keepdims=True) + EPS) * g


def _mem_kv_kernel(x_ref, g_ref, w_ref, o_ref):
    xn = _rms(x_ref[...], g_ref[...]).astype(BF16)
    o_ref[...] = _dot(xn, w_ref[0]).astype(BF16)


def _mem_kv(x, g, w, tm):
    rows, d = x.shape
    depth, _, n = w.shape
    return pl.pallas_call(
        _mem_kv_kernel,
        grid=(depth, rows // tm),
        in_specs=[pl.BlockSpec((tm, d), lambda l, i: (i, 0)), _whole((1, d)),
                  pl.BlockSpec((1, d, n), lambda l, i: (l, 0, 0))],
        out_specs=pl.BlockSpec((tm, n), lambda l, i: (i, l)),
        out_shape=jax.ShapeDtypeStruct((rows, depth * n), BF16),
        compiler_params=_params(("parallel", "parallel")),
        name="mem_kv_proj",
    )(x, g, w)


def _xattn_kernel(h_ref, g_ref, wq_ref, kv_ref, wo_ref, o_ref):
    d = h_ref.shape[1]
    dh = d // X_HEADS
    x = h_ref[...]
    xn = _rms(x, g_ref[...]).astype(BF16)
    q = _dot(xn, wq_ref[...]).astype(BF16)
    outs = []
    for hd in range(X_HEADS):
        qh = q[:, hd * dh:(hd + 1) * dh]
        kh = kv_ref[:, hd * dh:(hd + 1) * dh]
        vh = kv_ref[:, d + hd * dh:d + (hd + 1) * dh]
        s = _dot_nt(qh, kh) * (dh ** -0.5)
        e = jnp.exp(s - jnp.max(s, axis=-1, keepdims=True))
        p = e / jnp.sum(e, axis=-1, keepdims=True)
        outs.append(_dot(p.astype(BF16), vh))
    o = jnp.concatenate(outs, axis=-1).astype(BF16)
    o_ref[...] = x + _dot(o, wo_ref[...])


def _xattn(h, g, wq, kv, layer, wo, batch, mem_len):
    rows, d = h.shape
    tm = ROW_TILE
    per_batch = rows // batch // tm
    return pl.pallas_call(
        _xattn_kernel,
        grid=(batch, per_batch),
        in_specs=[
            pl.BlockSpec((tm, d), lambda b, i: (b * per_batch + i, 0)),
            _whole((1, d)),
            _whole((d, d)),
            pl.BlockSpec((mem_len, 2 * d), lambda b, i: (b, layer)),
            _whole((d, d)),
        ],
        out_specs=pl.BlockSpec((tm, d), lambda b, i: (b * per_batch + i, 0)),
        out_shape=jax.ShapeDtypeStruct((rows, d), F32),
        compiler_params=_params(("parallel", "parallel")),
        name="xattn",
    )(h, g, wq, kv, wo)


def _swiglu_kernel(h_ref, g_ref, wg_ref, wu_ref, wd_ref, gf_ref, o_ref, *, final_norm):
    x = h_ref[...]
    xn = _rms(x, g_ref[...]).astype(BF16)
    o_ref[...] = x
    for c in range(wg_ref.shape[1] // FF_CHUNK):
        cols = slice(c * FF_CHUNK, (c + 1) * FF_CHUNK)
        gate = _dot(xn, wg_ref[:, cols])
        up = _dot(xn, wu_ref[:, cols])
        act = (jax.nn.silu(gate) * up).astype(BF16)
        o_ref[...] += _dot(act, wd_ref[cols, :])
    if final_norm:
        o_ref[...] = _rms(o_ref[...], gf_ref[...])


def _swiglu(h, g, wg, wu, wd, gf, final_norm):
    rows, d = h.shape
    tm = FF_TILE
    return pl.pallas_call(
        functools.partial(_swiglu_kernel, final_norm=final_norm),
        grid=(rows // tm,),
        in_specs=[pl.BlockSpec((tm, d), lambda i: (i, 0)), _whole((1, d)), _whole(wg.shape),
                  _whole(wu.shape), _whole(wd.shape), _whole((1, d))],
        out_specs=pl.BlockSpec((tm, d), lambda i: (i, 0)),
        out_shape=jax.ShapeDtypeStruct((rows, d), F32),
        compiler_params=_params(("parallel",)),
        name="swiglu",
    )(h, g, wg, wu, wd, gf)


def _mlstm_project(h_ref, g_ref, wqk_ref, wv_ref, wo_ref, wgt_ref, cw_ref, gb_ref,
                   q_ref, k_ref, v_ref, og_ref, gr_ref, xp_ref):
    tm, d = h_ref.shape
    nqk = wqk_ref.shape[1]
    dk = nqk // (2 * A_HEADS)
    cw = cw_ref[...]
    xn_all = _rms(h_ref[...], g_ref[...]).astype(BF16)

    gr = _dot_nt(wgt_ref[...], xn_all) + gb_ref[...]
    is_input_gate = lax.broadcasted_iota(jnp.int32, gr.shape, 0) < A_HEADS
    gr = jnp.where(is_input_gate, gr, jax.nn.log_sigmoid(gr))
    pos = lax.broadcasted_iota(jnp.int32, gr.shape, 1) % MIX_CHUNK

    def chunk_scan(x, op):
        k = 1
        while k < MIX_CHUNK:
            x = jnp.where(pos >= k, op(x, pltpu.roll(x, k, axis=1)), x)
            k *= 2
        return x

    b = chunk_scan(gr, jnp.add)
    beta = pltpu.roll(gr, A_HEADS, axis=0) - b
    ng = gr.shape[0]
    gr_ref[0:ng, :] = beta
    gr_ref[ng:2 * ng, :] = b
    gr_ref[2 * ng:3 * ng, :] = chunk_scan(beta, jnp.maximum)

    for r0 in range(0, tm, SUB_TILE):
        rows = slice(r0, r0 + SUB_TILE)
        xn = xn_all[rows, :]
        qk = _dot(xn, wqk_ref[...])
        xp_ref[SUBLANES + r0:SUBLANES + r0 + SUB_TILE, :] = qk
        y = cw[A_CONV - 1:A_CONV, :] * qk
        for j in range(A_CONV - 1):
            y = y + cw[j:j + 1, :] * xp_ref[pl.ds(r0 + SUBLANES - (A_CONV - 1) + j, SUB_TILE), :]
        y = jax.nn.silu(y)
        q_ref[rows, :] = (y[:, :nqk // 2] * (dk ** -0.5)).astype(BF16)
        k_ref[rows, :] = y[:, nqk // 2:].astype(BF16)
        v_ref[rows, :] = _dot(xn, wv_ref[...]).astype(BF16)
        og_ref[rows, :] = jax.nn.sigmoid(_dot(xn, wo_ref[...])).astype(BF16)
    xp_ref[0:SUBLANES, :] = xp_ref[tm:tm + SUBLANES, :]


def _mlstm_recur(q_ref, k_ref, v_ref, og_ref, gr_ref, ng_ref, wout_ref, h_ref, o_ref,
                 cn_ref, m_ref, y_ref):
    ts = q_ref.shape[0]
    dk = q_ref.shape[1] // A_HEADS
    dv = v_ref.shape[1] // A_HEADS
    L = MIX_CHUNK
    H = A_HEADS
    causal = lax.broadcasted_iota(jnp.int32, (L, L), 1) <= lax.broadcasted_iota(jnp.int32, (L, L), 0)
    ones = jnp.ones((L, LANES), BF16)

    for c in range(ts // L):
        rows = slice(c * L, (c + 1) * L)
        beta = gr_ref[0:2 * H, rows]
        b_rows = gr_ref[2 * H:4 * H, rows]
        m_prev = m_ref[:, 0:1]
        big_m = jnp.maximum(m_prev, gr_ref[4 * H:6 * H, rows])
        m_last = big_m[:, L - 1:L]
        decay_all = jnp.exp(m_prev - m_last)
        m_ref[...] = jnp.broadcast_to(b_rows[:, L - 1:L] + m_last, m_ref.shape)
        m_cols = big_m.T
        w_inter_cols = jnp.exp(m_prev - big_m).T
        exp_neg_m_cols = jnp.exp(-(b_rows + big_m)).T
        wa_cols = jnp.exp(beta - m_last).T
        for hd in range(A_HEADS):
            j = H + hd
            qh = q_ref[rows, hd * dk:(hd + 1) * dk]
            kh = k_ref[rows, hd * dk:(hd + 1) * dk]
            v1 = jnp.concatenate([v_ref[rows, hd * dv:(hd + 1) * dv], ones], axis=1)
            cn_prev = cn_ref[hd]
            w_intra = jnp.exp(jnp.where(causal, beta[j:j + 1, :] - m_cols[:, j:j + 1], -jnp.inf))
            w_inter = w_inter_cols[:, j:j + 1]
            s = (_dot_nt(qh, kh) * w_intra).astype(BF16)
            both = _dot(s, v1) + w_inter * _dot(qh, cn_prev.astype(BF16))
            inv = 1.0 / jnp.maximum(jnp.abs(both[:, dv:]), exp_neg_m_cols[:, j:j + 1])
            hh = both[:, :dv] * jnp.concatenate([inv] * (dv // LANES), axis=1)
            kw = (kh.astype(F32) * wa_cols[:, j:j + 1]).astype(BF16)
            cn_ref[hd] = decay_all[j:j + 1, :] * cn_prev + _dot_tn(kw, v1)
            cols = slice(hd * dv, (hd + 1) * dv)
            yh = _head_rms(hh, ng_ref[:, cols]) * og_ref[rows, cols].astype(F32)
            y_ref[rows, cols] = yh.astype(BF16)

    o_ref[...] = h_ref[...] + _dot(y_ref[...], wout_ref[...])


def _mlstm_in_kernel(h_ref, g_ref, wqk_ref, wv_ref, wo_ref, wgt_ref, cw_ref, gb_ref,
                     q_ref, k_ref, v_ref, og_ref, gr_ref, xp_ref, *, tiles_per_seq):
    @pl.when(pl.program_id(0) % tiles_per_seq == 0)
    def _():
        xp_ref[0:SUBLANES, :] = jnp.zeros((SUBLANES, xp_ref.shape[1]), F32)

    _mlstm_project(h_ref, g_ref, wqk_ref, wv_ref, wo_ref, wgt_ref, cw_ref, gb_ref,
                   q_ref, k_ref, v_ref, og_ref, gr_ref, xp_ref)


def _mlstm_scan_kernel(q_ref, k_ref, v_ref, og_ref, gr_ref, ng_ref, wout_ref, h_ref, o_ref,
                       cn_ref, m_ref, y_ref):
    @pl.when(pl.program_id(1) == 0)
    def _():
        cn_ref[...] = jnp.zeros(cn_ref.shape, F32)
        m_ref[...] = jnp.zeros(m_ref.shape, F32)

    _mlstm_recur(q_ref, k_ref, v_ref, og_ref, gr_ref, ng_ref, wout_ref, h_ref, o_ref,
                 cn_ref, m_ref, y_ref)


def _mlstm_layer(h, gain, w_in, conv_w, gate_b, norm_g, w_out, batch):
    rows, d = h.shape
    seq = rows // batch
    nqk = conv_w.shape[1]
    dk = nqk // (2 * A_HEADS)
    dv = d // A_HEADS
    nv = A_HEADS * dv
    ng = 2 * A_HEADS
    wqk = w_in[:, :nqk].astype(BF16)
    wv = w_in[:, nqk:nqk + nv].astype(BF16)
    wo = w_in[:, nqk + nv:nqk + 2 * nv].astype(BF16)
    wgt = w_in[:, nqk + 2 * nv:].T.astype(BF16)
    tm = MLSTM_IN_TILE
    q, k, v, og, gr = pl.pallas_call(
        functools.partial(_mlstm_in_kernel, tiles_per_seq=seq // tm),
        grid=(rows // tm,),
        in_specs=[pl.BlockSpec((tm, d), lambda i: (i, 0)), _whole((1, d)), _whole((d, nqk)),
                  _whole((d, nv)), _whole((d, nv)), _whole((ng, d)), _whole((A_CONV, nqk)),
                  _whole((ng, 1))],
        out_specs=[pl.BlockSpec((tm, nqk // 2), lambda i: (i, 0)),
                   pl.BlockSpec((tm, nqk // 2), lambda i: (i, 0)),
                   pl.BlockSpec((tm, nv), lambda i: (i, 0)),
                   pl.BlockSpec((tm, nv), lambda i: (i, 0)),
                   pl.BlockSpec((3 * ng, tm), lambda i: (0, i))],
        out_shape=[jax.ShapeDtypeStruct((rows, nqk // 2), BF16),
                   jax.ShapeDtypeStruct((rows, nqk // 2), BF16),
                   jax.ShapeDtypeStruct((rows, nv), BF16),
                   jax.ShapeDtypeStruct((rows, nv), BF16),
                   jax.ShapeDtypeStruct((3 * ng, rows), F32)],
        scratch_shapes=[pltpu.VMEM((tm + SUBLANES, nqk), F32)],
        compiler_params=_params(("arbitrary",)),
        name="mlstm_in",
    )(h, gain, wqk, wv, wo, wgt, conv_w, gate_b.reshape(ng, 1))

    ts = MLSTM_TILE
    per_batch = seq // ts
    tile = lambda b, i: (b * per_batch + i, 0)
    return pl.pallas_call(
        _mlstm_scan_kernel,
        grid=(batch, per_batch),
        in_specs=[pl.BlockSpec((ts, nqk // 2), tile), pl.BlockSpec((ts, nqk // 2), tile),
                  pl.BlockSpec((ts, nv), tile), pl.BlockSpec((ts, nv), tile),
                  pl.BlockSpec((3 * ng, ts), lambda b, i: (0, b * per_batch + i)),
                  _whole((1, nv)), _whole((nv, d)), pl.BlockSpec((ts, d), tile)],
        out_specs=pl.BlockSpec((ts, d), tile),
        out_shape=jax.ShapeDtypeStruct((rows, d), F32),
        scratch_shapes=[pltpu.VMEM((A_HEADS, dk, dv + LANES), F32),
                        pltpu.VMEM((ng, LANES), F32),
                        pltpu.VMEM((ts, nv), BF16)],
        compiler_params=_params(("parallel", "arbitrary")),
        name="mlstm_scan",
    )(q, k, v, og, gr, norm_g.reshape(1, nv), w_out.astype(BF16), h)


def _gla_in_kernel(h_ref, g_ref, wq_ref, wk_ref, wv_ref, wr_ref, wg1_ref, wg2_ref, gb_ref,
                   q_ref, k_ref, v_ref, rg_ref, la_ref):
    dk = wq_ref.shape[1] // B_HEADS
    xn = _rms(h_ref[...], g_ref[...]).astype(BF16)
    q_ref[...] = (_dot(xn, wq_ref[...]) * (dk ** -0.5)).astype(BF16)
    k_ref[...] = _dot(xn, wk_ref[...]).astype(BF16)
    v_ref[...] = _dot(xn, wv_ref[...]).astype(BF16)
    rg_ref[...] = jax.nn.silu(_dot(xn, wr_ref[...])).astype(BF16)
    g_low = _dot(xn, wg1_ref[...]).astype(BF16)
    la_ref[...] = jax.nn.log_sigmoid(_dot(g_low, wg2_ref[...]) + gb_ref[...]) / B_TAU


def _bcast_sublane(x3, idx):
    return jnp.broadcast_to(x3[:, idx:idx + 1, :], x3.shape)


def _boundary_rows(b, half):
    L, w = b.shape
    if half >= SUBLANES:
        nb = L // (2 * half)
        b4 = b.reshape(nb, 2 * half // SUBLANES, SUBLANES, w)
        last = b4[:, half // SUBLANES - 1, SUBLANES - 1:SUBLANES, :]
        return jnp.broadcast_to(last[:, None, :, :], b4.shape).reshape(L, w)
    b3 = b.reshape(L // SUBLANES, SUBLANES, w)
    sub = lax.broadcasted_iota(jnp.int32, b3.shape, 1)
    out = None
    for blk in range(SUBLANES // (2 * half)):
        cand = _bcast_sublane(b3, blk * 2 * half + half - 1)
        out = cand if out is None else jnp.where(sub >= blk * 2 * half, cand, out)
    return out.reshape(L, w)


def _gla_scan_kernel(q_ref, k_ref, v_ref, rg_ref, la_ref, ng_ref, wout_ref, h_ref, o_ref,
                     st_ref, y_ref):
    ts = q_ref.shape[0]
    dk = q_ref.shape[1] // B_HEADS
    dv = v_ref.shape[1] // B_HEADS
    L = GLA_CHUNK

    @pl.when(pl.program_id(1) == 0)
    def _():
        st_ref[...] = jnp.zeros(st_ref.shape, F32)

    row = lax.broadcasted_iota(jnp.int32, (L, L), 0)
    col = lax.broadcasted_iota(jnp.int32, (L, L), 1)
    tri = (col <= row).astype(BF16)
    eye = (col == row).astype(F32)
    eye_k = (lax.broadcasted_iota(jnp.int32, (dk, dk), 0) == lax.broadcasted_iota(jnp.int32, (dk, dk), 1))

    levels = []
    half = L // 2
    while half >= 1:
        pairs = ((row // (2 * half)) == (col // (2 * half))) & ((row & half) != 0) & ((col & half) == 0)
        levels.append((half, pairs.astype(F32)))
        half //= 2

    for c in range(ts // L):
        rows = slice(c * L, (c + 1) * L)
        b_all = _dot_exact_lhs01(tri, la_ref[rows, :])
        for hd in range(B_HEADS):
            kc = slice(hd * dk, (hd + 1) * dk)
            vc = slice(hd * dv, (hd + 1) * dv)
            b = b_all[:, kc]
            qh = q_ref[rows, kc].astype(F32)
            kh = k_ref[rows, kc].astype(F32)
            vh = v_ref[rows, vc]
            s_mat = eye * _dot_nt(qh.astype(BF16), kh.astype(BF16))
            for half, pairs in levels:
                e = jnp.exp(-jnp.abs(b - _boundary_rows(b, half)))
                s_mat = s_mat + pairs * _dot_nt((qh * e).astype(BF16), (kh * e).astype(BF16))
            state = st_ref[hd]
            o = _dot(s_mat.astype(BF16), vh) + _dot((qh * jnp.exp(b)).astype(BF16), state.astype(BF16))
            b_last = b[L - 1:L, :]
            k_dec = kh * jnp.exp(b_last - b)
            decay_col = jnp.sum(jnp.where(eye_k, jnp.exp(b_last), 0.0), axis=1, keepdims=True)
            st_ref[hd] = decay_col * state + _dot_tn(k_dec.astype(BF16), vh)
            yh = _head_rms(o, ng_ref[:, vc]) * rg_ref[rows, vc].astype(F32)
            y_ref[rows, vc] = yh.astype(BF16)

    o_ref[...] = h_ref[...] + _dot(y_ref[...], wout_ref[...])


def _gla_layer(h, gain, w_in, gate_w2, gate_b, norm_g, w_out, batch):
    rows, d = h.shape
    seq = rows // batch
    nk = gate_w2.shape[1]
    dk = nk // B_HEADS
    dv = d // B_HEADS
    nv = B_HEADS * dv
    wq = w_in[:, :nk].astype(BF16)
    wk = w_in[:, nk:2 * nk].astype(BF16)
    wv = w_in[:, 2 * nk:2 * nk + nv].astype(BF16)
    wr = w_in[:, 2 * nk + nv:2 * nk + 2 * nv].astype(BF16)
    wg1 = jnp.pad(w_in[:, 2 * nk + 2 * nv:], ((0, 0), (0, LANES - B_RANK))).astype(BF16)
    wg2 = jnp.pad(gate_w2, ((0, LANES - B_RANK), (0, 0))).astype(BF16)
    tm = ROW_TILE
    row_tile = lambda i: (i, 0)
    q, k, v, rg, la = pl.pallas_call(
        _gla_in_kernel,
        grid=(rows // tm,),
        in_specs=[pl.BlockSpec((tm, d), row_tile), _whole((1, d)), _whole((d, nk)), _whole((d, nk)),
                  _whole((d, nv)), _whole((d, nv)), _whole((d, LANES)), _whole((LANES, nk)),
                  _whole((1, nk))],
        out_specs=[pl.BlockSpec((tm, nk), row_tile), pl.BlockSpec((tm, nk), row_tile),
                   pl.BlockSpec((tm, nv), row_tile), pl.BlockSpec((tm, nv), row_tile),
                   pl.BlockSpec((tm, nk), row_tile)],
        out_shape=[jax.ShapeDtypeStruct((rows, nk), BF16), jax.ShapeDtypeStruct((rows, nk), BF16),
                   jax.ShapeDtypeStruct((rows, nv), BF16), jax.ShapeDtypeStruct((rows, nv), BF16),
                   jax.ShapeDtypeStruct((rows, nk), F32)],
        compiler_params=_params(("parallel",)),
        name="gla_in",
    )(h, gain, wq, wk, wv, wr, wg1, wg2, gate_b.reshape(1, nk))

    ts = GLA_TILE
    per_batch = seq // ts
    tile = lambda b, i: (b * per_batch + i, 0)
    return pl.pallas_call(
        _gla_scan_kernel,
        grid=(batch, per_batch),
        in_specs=[pl.BlockSpec((ts, nk), tile), pl.BlockSpec((ts, nk), tile),
                  pl.BlockSpec((ts, nv), tile), pl.BlockSpec((ts, nv), tile),
                  pl.BlockSpec((ts, nk), tile), _whole((1, nv)), _whole((nv, d)),
                  pl.BlockSpec((ts, d), tile)],
        out_specs=pl.BlockSpec((ts, d), tile),
        out_shape=jax.ShapeDtypeStruct((rows, d), F32),
        scratch_shapes=[pltpu.VMEM((B_HEADS, dk, dv), F32), pltpu.VMEM((ts, nv), BF16)],
        compiler_params=_params(("parallel", "arbitrary")),
        name="gla_scan",
    )(q, k, v, rg, la, norm_g.reshape(1, nv), w_out.astype(BF16), h)


def _s5_kernel(h_ref, g_ref, win_ref, perm_ref, permt_ref, bmat_ref, acoef_ref, apow_ref, atile_ref,
               cmat_ref, dskip_ref, wg_ref, bg_ref, wo_ref, o_ref, carry_ref, bu_ref, s_ref, sb_ref, yp_ref):
    ts, d = h_ref.shape
    nblk, bw = bmat_ref.shape[0], bmat_ref.shape[1]
    half = bmat_ref.shape[2] // 2
    nj = ts // SUBLANES

    @pl.when(pl.program_id(1) == 0)
    def _():
        carry_ref[...] = jnp.zeros(carry_ref.shape, F32)

    xn = _rms(h_ref[...], g_ref[...]).astype(BF16)
    xp = _dot(perm_ref[...], xn).astype(BF16)
    u = _dot(xp, win_ref[...])
    ub = u.astype(BF16)

    for blk in range(nblk):
        bu_ref[...] = _dot(ub[:, blk * bw:(blk + 1) * bw], bmat_ref[blk])
        a_re = acoef_ref[blk, 0]
        a_im = acoef_ref[blk, 1]

        def local_step(j, x):
            x_re, x_im = x
            r = pl.ds(pl.multiple_of(j * SUBLANES, SUBLANES), SUBLANES)
            n_re = a_re * x_re - a_im * x_im + bu_ref[r, 0:half]
            n_im = a_re * x_im + a_im * x_re + bu_ref[r, half:2 * half]
            s_ref[r, 0:half] = n_re
            s_ref[r, half:2 * half] = n_im
            return n_re, n_im

        zero = jnp.zeros((SUBLANES, half), F32)
        end_re, end_im = lax.fori_loop(0, nj, local_step, (zero, zero), unroll=S5_UNROLL)

        t_re = atile_ref[blk, 0]
        t_im = atile_ref[blk, 1]
        c_re = carry_ref[blk, 0:1, 0:half]
        c_im = carry_ref[blk, 0:1, half:2 * half]
        cin_re, cin_im = [], []
        for r in range(SUBLANES):
            cin_re.append(c_re)
            cin_im.append(c_im)
            e_re = end_re[r:r + 1, :] + t_re * c_re - t_im * c_im
            e_im = end_im[r:r + 1, :] + t_re * c_im + t_im * c_re
            c_re, c_im = e_re, e_im
        carry_ref[blk, :, 0:half] = jnp.broadcast_to(c_re, (SUBLANES, half))
        carry_ref[blk, :, half:2 * half] = jnp.broadcast_to(c_im, (SUBLANES, half))
        cin_re = jnp.concatenate(cin_re, axis=0)
        cin_im = jnp.concatenate(cin_im, axis=0)

        def fix_step(jj, _):
            out_re, out_im = [], []
            for k in range(2):
                j = 2 * jj + k
                r = pl.ds(pl.multiple_of(j * SUBLANES, SUBLANES), SUBLANES)
                p_re = apow_ref[blk, 0, pl.ds(j, 1), :]
                p_im = apow_ref[blk, 1, pl.ds(j, 1), :]
                out_re.append(s_ref[r, 0:half] + p_re * cin_re - p_im * cin_im)
                out_im.append(s_ref[r, half:2 * half] + p_re * cin_im + p_im * cin_re)
            r2 = pl.ds(pl.multiple_of(jj * 2 * SUBLANES, 2 * SUBLANES), 2 * SUBLANES)
            sb_ref[r2, 0:half] = jnp.concatenate(out_re, axis=0).astype(BF16)
            sb_ref[r2, half:2 * half] = jnp.concatenate(out_im, axis=0).astype(BF16)
            return 0

        lax.fori_loop(0, nj // 2, fix_step, 0, unroll=S5_UNROLL // 2)
        yp_ref[:, blk * bw:(blk + 1) * bw] = _dot(sb_ref[...], cmat_ref[blk])

    y = jax.nn.gelu(yp_ref[...] + dskip_ref[...] * u)
    z = y * jax.nn.sigmoid(_dot(y.astype(BF16), wg_ref[...]) + bg_ref[...])
    z = _dot(permt_ref[...], z.astype(BF16)).astype(BF16)
    o_ref[...] = h_ref[...] + _dot(z, wo_ref[...])


def _s5_discretise(lam_re, lam_im, log_dt, b_re, b_im, c_re, c_im, ts):
    groups, states = lam_re.shape
    gpb = S5_BLOCK_GROUPS
    nblk = groups // gpb
    lr = lam_re.astype(F32)
    li = lam_im.astype(F32)
    dt = jnp.exp(log_dt.astype(F32))[:, None]
    mag = jnp.exp(lr * dt)
    ab_re = mag * jnp.cos(li * dt)
    ab_im = mag * jnp.sin(li * dt)
    den = lr * lr + li * li
    z_re = ((ab_re - 1.0) * lr + ab_im * li) / den
    z_im = (ab_im * lr - (ab_re - 1.0) * li) / den
    br = b_re.astype(F32)
    bi = b_im.astype(F32)
    bb_re = z_re[..., None] * br - z_im[..., None] * bi
    bb_im = z_re[..., None] * bi + z_im[..., None] * br
    eye = jnp.eye(gpb, dtype=F32)

    def in_block(bb):
        t = bb.reshape(nblk, gpb, states, C_GROUP)
        return jnp.einsum('ngph,gk->nghkp', t, eye).reshape(nblk, gpb * C_GROUP, gpb * states)

    bmat = jnp.concatenate([in_block(bb_re), in_block(bb_im)], axis=-1).astype(BF16)

    def out_block(cc):
        t = cc.astype(F32).reshape(nblk, gpb, C_GROUP, states)
        return jnp.einsum('nghp,gk->ngpkh', t, eye).reshape(nblk, gpb * states, gpb * C_GROUP)

    cmat = jnp.concatenate([out_block(c_re), -out_block(c_im)], axis=1).astype(BF16)

    def powers(n):
        n = n.astype(F32)[:, None, None]
        m = jnp.exp(n * (lr * dt)[None])
        ang = n * (li * dt)[None]
        pr = (m * jnp.cos(ang)).reshape(-1, nblk, gpb * states).transpose(1, 0, 2)
        pi = (m * jnp.sin(ang)).reshape(-1, nblk, gpb * states).transpose(1, 0, 2)
        return jnp.stack([pr, pi], axis=1)

    nj = ts // SUBLANES
    acoef = jnp.broadcast_to(powers(jnp.ones((1,)))[:, :, 0:1, :], (nblk, 2, SUBLANES, gpb * states))
    apow = powers(jnp.arange(1, nj + 1))
    atile = powers(jnp.full((1,), nj))
    return bmat, cmat, acoef, apow, atile


def _s5_layer(h, gain, w_in, lam_re, lam_im, log_dt, b_re, b_im, c_re, c_im, d_skip, w_gate, b_gate,
              w_out, batch):
    rows, d = h.shape
    seq = rows // batch
    ts = S5_TILE
    nj = ts // SUBLANES
    per_batch = seq // ts
    bmat, cmat, acoef, apow, atile = _s5_discretise(lam_re, lam_im, log_dt, b_re, b_im, c_re, c_im, ts)
    nblk, _, sw = bmat.shape
    p = jnp.arange(ts)
    perm = (p[None, :] == ((p % SUBLANES) * nj + p // SUBLANES)[:, None]).astype(BF16)
    tile = lambda b, i: (b * per_batch + i, 0)
    return pl.pallas_call(
        _s5_kernel,
        grid=(batch, per_batch),
        in_specs=[pl.BlockSpec((ts, d), tile), _whole((1, d)), _whole((d, d)), _whole((ts, ts)), _whole((ts, ts)),
                  _whole(bmat.shape), _whole(acoef.shape), _whole(apow.shape), _whole(atile.shape),
                  _whole(cmat.shape), _whole((1, d)), _whole((d, d)), _whole((1, d)), _whole((d, d))],
        out_specs=pl.BlockSpec((ts, d), tile),
        out_shape=jax.ShapeDtypeStruct((rows, d), F32),
        scratch_shapes=[pltpu.VMEM((nblk, SUBLANES, sw), F32), pltpu.VMEM((ts, sw), F32),
                        pltpu.VMEM((ts, sw), F32), pltpu.VMEM((ts, sw), BF16),
                        pltpu.VMEM((ts, d), F32)],
        compiler_params=_params(("parallel", "arbitrary")),
        name="s5_mix",
    )(h, gain, w_in.astype(BF16), perm, perm.T, bmat, acoef, apow, atile, cmat, d_skip.reshape(1, d),
      w_gate.astype(BF16), b_gate.reshape(1, d), w_out.astype(BF16))


def kernel(x, mem, norm_mix, norm_x, norm_ffn, norm_mem, norm_final, a_w_in, a_conv_w, a_gate_b, a_norm, a_w_out, b_w_in, b_gate_w2, b_gate_b, b_norm, b_w_out, c_w_in, c_lam_re, c_lam_im, c_log_dt, c_b_re, c_b_im, c_c_re, c_c_im, c_d, c_w_gate, c_b_gate, c_w_out, x_w_q, x_w_kv, x_w_o, f_w_gate, f_w_up, f_w_down):
    batch, seq, d = x.shape
    mem_len = mem.shape[1]
    depth = norm_mix.shape[0]
    assert seq % ROW_TILE == 0 and seq % MLSTM_TILE == 0 and seq % GLA_TILE == 0 and seq % S5_TILE == 0
    assert MLSTM_TILE % MIX_CHUNK == 0 and GLA_TILE % GLA_CHUNK == 0 and seq % MLSTM_IN_TILE == 0
    assert f_w_gate.shape[2] % FF_CHUNK == 0 and d % (X_HEADS * LANES) == 0 and (batch * seq) % FF_TILE == 0

    kv = _mem_kv(mem.reshape(batch * mem_len, d), norm_mem.reshape(1, d), x_w_kv.astype(BF16), mem_len)

    h = x.reshape(batch * seq, d)
    for i in range(depth):
        kind = i % N_MIXERS
        j = i // N_MIXERS
        gain = norm_mix[i].reshape(1, d)
        if kind == 0:
            h = _mlstm_layer(h, gain, a_w_in[j], a_conv_w[j], a_gate_b[j], a_norm[j], a_w_out[j], batch)
        elif kind == 1:
            h = _gla_layer(h, gain, b_w_in[j], b_gate_w2[j], b_gate_b[j], b_norm[j], b_w_out[j], batch)
        else:
            h = _s5_layer(h, gain, c_w_in[j], c_lam_re[j], c_lam_im[j], c_log_dt[j], c_b_re[j], c_b_im[j],
                          c_c_re[j], c_c_im[j], c_d[j], c_w_gate[j], c_b_gate[j], c_w_out[j], batch)
        h = _xattn(h, norm_x[i].reshape(1, d), x_w_q[i].astype(BF16), kv, i, x_w_o[i].astype(BF16),
                   batch, mem_len)
        h = _swiglu(h, norm_ffn[i].reshape(1, d), f_w_gate[i].astype(BF16), f_w_up[i].astype(BF16),
                    f_w_down[i].astype(BF16), norm_final.reshape(1, d), final_norm=(i == depth - 1))
    return h.reshape(batch, seq, d)
```

```python
import functools
import math

import jax
import jax.numpy as jnp
from jax import lax
from jax.experimental import pallas as pl
from jax.experimental.pallas import tpu as pltpu

F32 = jnp.float32
BF16 = jnp.bfloat16
EPS = 1e-6

N_MIXERS = 3
A_HEADS = 4
A_CONV = 4
B_HEADS = 4
B_RANK = 16
B_TAU = 16.0
C_GROUP = 16
C_STATE = 64
X_HEADS = 4

LANES = 128
SUBLANES = 8
VMEM_LIMIT_BYTES = 56 * 1024 * 1024

ROW_TILE = 1024
FF_TILE = 1024
SUB_TILE = 256
MLSTM_IN_TILE = 1024
MLSTM_TILE = 1024
GLA_TILE = 1024
MIX_CHUNK = 512
GLA_CHUNK = 128
S5_TILE = 512
S5_UNROLL = S5_TILE // SUBLANES
FF_CHUNK = 256
S5_BLOCK_GROUPS = 2 * LANES // C_GROUP


def _params(semantics):
    return pltpu.CompilerParams(dimension_semantics=semantics, vmem_limit_bytes=VMEM_LIMIT_BYTES)


def _whole(shape, single=False):
    zeros = (0,) * len(shape)
    if single:
        return pl.BlockSpec(shape, lambda *_: zeros, pipeline_mode=pl.Buffered(1))
    return pl.BlockSpec(shape, lambda *_: zeros)


def _rms(x, g):
    return x * lax.rsqrt(jnp.mean(x * x, axis=-1, keepdims=True) + EPS) * g


def _dot(a, b):
    return jnp.dot(a, b, preferred_element_type=F32)


def _dot_nt(a, b):
    return lax.dot_general(a, b, (((1,), (1,)), ((), ())), preferred_element_type=F32)


def _dot_tn(a, b):
    return lax.dot_general(a, b, (((0,), (0,)), ((), ())), preferred_element_type=F32)


def _split3(x):
    hi = x.astype(BF16)
    r = x - hi.astype(F32)
    mid = r.astype(BF16)
    lo = (r - mid.astype(F32)).astype(BF16)
    return hi, mid, lo


def _dot_exact_lhs01(m01, x):
    hi, mid, lo = _split3(x)
    return _dot(m01, hi) + _dot(m01, mid) + _dot(m01, lo)


def _head_rms(o, g):
    return o * lax.rsqrt(jnp.mean(o * o, axis=-1, keepdims=True) + EPS) * g


def _mem_kv_kernel(x_ref, g_ref, w_ref, o_ref):
    xn = _rms(x_ref[...], g_ref[...]).astype(BF16)
    o_ref[...] = _dot(xn, w_ref[0]).astype(BF16)


def _mem_kv(x, g, w, tm):
    rows, d = x.shape
    depth, _, n = w.shape
    return pl.pallas_call(
        _mem_kv_kernel,
        grid=(depth, rows // tm),
        in_specs=[pl.BlockSpec((tm, d), lambda l, i: (i, 0)), _whole((1, d)),
                  pl.BlockSpec((1, d, n), lambda l, i: (l, 0, 0))],
        out_specs=pl.BlockSpec((tm, n), lambda l, i: (i, l)),
        out_shape=jax.ShapeDtypeStruct((rows, depth * n), BF16),
        compiler_params=_params(("parallel", "parallel")),
        name="mem_kv_proj",
    )(x, g, w)


def _xattn_kernel(h_ref, g_ref, wq_ref, kv_ref, wo_ref, o_ref):
    d = h_ref.shape[1]
    dh = d // X_HEADS
    x = h_ref[...]
    xn = _rms(x, g_ref[...]).astype(BF16)
    q = _dot(xn, wq_ref[...]).astype(BF16)
    outs = []
    for hd in range(X_HEADS):
        qh = q[:, hd * dh:(hd + 1) * dh]
        kh = kv_ref[:, hd * dh:(hd + 1) * dh]
        vh = kv_ref[:, d + hd * dh:d + (hd + 1) * dh]
        s = _dot_nt(qh, kh) * (dh ** -0.5)
        e = jnp.exp(s - jnp.max(s, axis=-1, keepdims=True))
        p = e / jnp.sum(e, axis=-1, keepdims=True)
        outs.append(_dot(p.astype(BF16), vh))
    o = jnp.concatenate(outs, axis=-1).astype(BF16)
    o_ref[...] = x + _dot(o, wo_ref[...])


def _xattn(h, g, wq, kv, layer, wo, batch, mem_len):
    rows, d = h.shape
    tm = ROW_TILE
    per_batch = rows // batch // tm
    return pl.pallas_call(
        _xattn_kernel,
        grid=(batch, per_batch),
        in_specs=[
            pl.BlockSpec((tm, d), lambda b, i: (b * per_batch + i, 0)),
            _whole((1, d)),
            _whole((d, d)),
            pl.BlockSpec((mem_len, 2 * d), lambda b, i: (b, layer)),
            _whole((d, d)),
        ],
        out_specs=pl.BlockSpec((tm, d), lambda b, i: (b * per_batch + i, 0)),
        out_shape=jax.ShapeDtypeStruct((rows, d), F32),
        compiler_params=_params(("parallel", "parallel")),
        name="xattn",
    )(h, g, wq, kv, wo)


def _swiglu_kernel(h_ref, g_ref, wg_ref, wu_ref, wd_ref, gf_ref, o_ref, *, final_norm):
    x = h_ref[...]
    xn = _rms(x, g_ref[...]).astype(BF16)
    o_ref[...] = x
    for c in range(wg_ref.shape[1] // FF_CHUNK):
        cols = slice(c * FF_CHUNK, (c + 1) * FF_CHUNK)
        gate = _dot(xn, wg_ref[:, cols])
        up = _dot(xn, wu_ref[:, cols])
        act = (jax.nn.silu(gate) * up).astype(BF16)
        o_ref[...] += _dot(act, wd_ref[cols, :])
    if final_norm:
        o_ref[...] = _rms(o_ref[...], gf_ref[...])


def _swiglu(h, g, wg, wu, wd, gf, final_norm):
    rows, d = h.shape
    tm = FF_TILE
    return pl.pallas_call(
        functools.partial(_swiglu_kernel, final_norm=final_norm),
        grid=(rows // tm,),
        in_specs=[pl.BlockSpec((tm, d), lambda i: (i, 0)), _whole((1, d)), _whole(wg.shape, single=True),
                  _whole(wu.shape, single=True), _whole(wd.shape, single=True), _whole((1, d))],
        out_specs=pl.BlockSpec((tm, d), lambda i: (i, 0)),
        out_shape=jax.ShapeDtypeStruct((rows, d), F32),
        compiler_params=_params(("parallel",)),
        name="swiglu",
    )(h, g, wg, wu, wd, gf)


def _mlstm_project(h_ref, g_ref, wqk_ref, wv_ref, wo_ref, wgt_ref, cw_ref, gb_ref,
                   q_ref, k_ref, v_ref, og_ref, gr_ref, xp_ref):
    tm, d = h_ref.shape
    nqk = wqk_ref.shape[1]
    dk = nqk // (2 * A_HEADS)
    cw = cw_ref[...]
    xn_all = _rms(h_ref[...], g_ref[...]).astype(BF16)

    gr = _dot_nt(wgt_ref[...], xn_all) + gb_ref[...]
    is_input_gate = lax.broadcasted_iota(jnp.int32, gr.shape, 0) < A_HEADS
    gr = jnp.where(is_input_gate, gr, jax.nn.log_sigmoid(gr))
    pos = lax.broadcasted_iota(jnp.int32, gr.shape, 1) % MIX_CHUNK

    def chunk_scan(x, op):
        k = 1
        while k < MIX_CHUNK:
            x = jnp.where(pos >= k, op(x, pltpu.roll(x, k, axis=1)), x)
            k *= 2
        return x

    b = chunk_scan(gr, jnp.add)
    beta = pltpu.roll(gr, A_HEADS, axis=0) - b
    ng = gr.shape[0]
    gr_ref[0:ng, :] = beta
    gr_ref[ng:2 * ng, :] = b
    gr_ref[2 * ng:3 * ng, :] = chunk_scan(beta, jnp.maximum)

    for r0 in range(0, tm, SUB_TILE):
        rows = slice(r0, r0 + SUB_TILE)
        xn = xn_all[rows, :]
        qk = _dot(xn, wqk_ref[...])
        xp_ref[SUBLANES + r0:SUBLANES + r0 + SUB_TILE, :] = qk
        y = cw[A_CONV - 1:A_CONV, :] * qk
        for j in range(A_CONV - 1):
            y = y + cw[j:j + 1, :] * xp_ref[pl.ds(r0 + SUBLANES - (A_CONV - 1) + j, SUB_TILE), :]
        y = jax.nn.silu(y)
        q_ref[rows, :] = (y[:, :nqk // 2] * (dk ** -0.5)).astype(BF16)
        k_ref[rows, :] = y[:, nqk // 2:].astype(BF16)
        v_ref[rows, :] = _dot(xn, wv_ref[...]).astype(BF16)
        og_ref[rows, :] = jax.nn.sigmoid(_dot(xn, wo_ref[...])).astype(BF16)
    xp_ref[0:SUBLANES, :] = xp_ref[tm:tm + SUBLANES, :]


def _mlstm_recur(q_ref, k_ref, v_ref, og_ref, gr_ref, ng_ref, wout_ref, h_ref, o_ref,
                 cn_ref, m_ref, y_ref):
    ts = q_ref.shape[0]
    dk = q_ref.shape[1] // A_HEADS
    dv = v_ref.shape[1] // A_HEADS
    L = MIX_CHUNK
    H = A_HEADS
    causal = lax.broadcasted_iota(jnp.int32, (L, L), 1) <= lax.broadcasted_iota(jnp.int32, (L, L), 0)
    ones = jnp.ones((L, LANES), BF16)

    for c in range(ts // L):
        rows = slice(c * L, (c + 1) * L)
        beta = gr_ref[0:2 * H, rows]
        b_rows = gr_ref[2 * H:4 * H, rows]
        m_prev = m_ref[:, 0:1]
        big_m = jnp.maximum(m_prev, gr_ref[4 * H:6 * H, rows])
        m_last = big_m[:, L - 1:L]
        decay_all = jnp.exp(m_prev - m_last)
        m_ref[...] = jnp.broadcast_to(b_rows[:, L - 1:L] + m_last, m_ref.shape)
        m_cols = big_m.T
        w_inter_cols = jnp.exp(m_prev - big_m).T
        exp_neg_m_cols = jnp.exp(-(b_rows + big_m)).T
        wa_cols = jnp.exp(beta - m_last).T
        for hd in range(A_HEADS):
            j = H + hd
            qh = q_ref[rows, hd * dk:(hd + 1) * dk]
            kh = k_ref[rows, hd * dk:(hd + 1) * dk]
            v1 = jnp.concatenate([v_ref[rows, hd * dv:(hd + 1) * dv], ones], axis=1)
            cn_prev = cn_ref[hd]
            w_intra = jnp.exp(jnp.where(causal, beta[j:j + 1, :] - m_cols[:, j:j + 1], -jnp.inf))
            w_inter = w_inter_cols[:, j:j + 1]
            s = (_dot_nt(qh, kh) * w_intra).astype(BF16)
            both = _dot(s, v1) + w_inter * _dot(qh, cn_prev.astype(BF16))
            inv = 1.0 / jnp.maximum(jnp.abs(both[:, dv:]), exp_neg_m_cols[:, j:j + 1])
            hh = both[:, :dv] * jnp.concatenate([inv] * (dv // LANES), axis=1)
            kw = (kh.astype(F32) * wa_cols[:, j:j + 1]).astype(BF16)
            cn_ref[hd] = decay_all[j:j + 1, :] * cn_prev + _dot_tn(kw, v1)
            cols = slice(hd * dv, (hd + 1) * dv)
            yh = _head_rms(hh, ng_ref[:, cols]) * og_ref[rows, cols].astype(F32)
            y_ref[rows, cols] = yh.astype(BF16)

    o_ref[...] = h_ref[...] + _dot(y_ref[...], wout_ref[...])


def _mlstm_in_kernel(h_ref, g_ref, wqk_ref, wv_ref, wo_ref, wgt_ref, cw_ref, gb_ref,
                     q_ref, k_ref, v_ref, og_ref, gr_ref, xp_ref, *, tiles_per_seq):
    @pl.when(pl.program_id(0) % tiles_per_seq == 0)
    def _():
        xp_ref[0:SUBLANES, :] = jnp.zeros((SUBLANES, xp_ref.shape[1]), F32)

    _mlstm_project(h_ref, g_ref, wqk_ref, wv_ref, wo_ref, wgt_ref, cw_ref, gb_ref,
                   q_ref, k_ref, v_ref, og_ref, gr_ref, xp_ref)


def _mlstm_scan_kernel(q_ref, k_ref, v_ref, og_ref, gr_ref, ng_ref, wout_ref, h_ref, o_ref,
                       cn_ref, m_ref, y_ref):
    @pl.when(pl.program_id(1) == 0)
    def _():
        cn_ref[...] = jnp.zeros(cn_ref.shape, F32)
        m_ref[...] = jnp.zeros(m_ref.shape, F32)

    _mlstm_recur(q_ref, k_ref, v_ref, og_ref, gr_ref, ng_ref, wout_ref, h_ref, o_ref,
                 cn_ref, m_ref, y_ref)


def _mlstm_layer(h, gain, w_in, conv_w, gate_b, norm_g, w_out, batch):
    rows, d = h.shape
    seq = rows // batch
    nqk = conv_w.shape[1]
    dk = nqk // (2 * A_HEADS)
    dv = d // A_HEADS
    nv = A_HEADS * dv
    ng = 2 * A_HEADS
    wqk = w_in[:, :nqk].astype(BF16)
    wv = w_in[:, nqk:nqk + nv].astype(BF16)
    wo = w_in[:, nqk + nv:nqk + 2 * nv].astype(BF16)
    wgt = w_in[:, nqk + 2 * nv:].T.astype(BF16)
    tm = MLSTM_IN_TILE
    q, k, v, og, gr = pl.pallas_call(
        functools.partial(_mlstm_in_kernel, tiles_per_seq=seq // tm),
        grid=(rows // tm,),
        in_specs=[pl.BlockSpec((tm, d), lambda i: (i, 0)), _whole((1, d)), _whole((d, nqk)),
                  _whole((d, nv)), _whole((d, nv)), _whole((ng, d)), _whole((A_CONV, nqk)),
                  _whole((ng, 1))],
        out_specs=[pl.BlockSpec((tm, nqk // 2), lambda i: (i, 0)),
                   pl.BlockSpec((tm, nqk // 2), lambda i: (i, 0)),
                   pl.BlockSpec((tm, nv), lambda i: (i, 0)),
                   pl.BlockSpec((tm, nv), lambda i: (i, 0)),
                   pl.BlockSpec((3 * ng, tm), lambda i: (0, i))],
        out_shape=[jax.ShapeDtypeStruct((rows, nqk // 2), BF16),
                   jax.ShapeDtypeStruct((rows, nqk // 2), BF16),
                   jax.ShapeDtypeStruct((rows, nv), BF16),
                   jax.ShapeDtypeStruct((rows, nv), BF16),
                   jax.ShapeDtypeStruct((3 * ng, rows), F32)],
        scratch_shapes=[pltpu.VMEM((tm + SUBLANES, nqk), F32)],
        compiler_params=_params(("arbitrary",)),
        name="mlstm_in",
    )(h, gain, wqk, wv, wo, wgt, conv_w, gate_b.reshape(ng, 1))

    ts = MLSTM_TILE
    per_batch = seq // ts
    tile = lambda b, i: (b * per_batch + i, 0)
    return pl.pallas_call(
        _mlstm_scan_kernel,
        grid=(batch, per_batch),
        in_specs=[pl.BlockSpec((ts, nqk // 2), tile), pl.BlockSpec((ts, nqk // 2), tile),
                  pl.BlockSpec((ts, nv), tile), pl.BlockSpec((ts, nv), tile),
                  pl.BlockSpec((3 * ng, ts), lambda b, i: (0, b * per_batch + i)),
                  _whole((1, nv)), _whole((nv, d)), pl.BlockSpec((ts, d), tile)],
        out_specs=pl.BlockSpec((ts, d), tile),
        out_shape=jax.ShapeDtypeStruct((rows, d), F32),
        scratch_shapes=[pltpu.VMEM((A_HEADS, dk, dv + LANES), F32),
                        pltpu.VMEM((ng, LANES), F32),
                        pltpu.VMEM((ts, nv), BF16)],
        compiler_params=_params(("parallel", "arbitrary")),
        name="mlstm_scan",
    )(q, k, v, og, gr, norm_g.reshape(1, nv), w_out.astype(BF16), h)


def _gla_in_kernel(h_ref, g_ref, wq_ref, wk_ref, wv_ref, wr_ref, wg1_ref, wg2_ref, gb_ref,
                   q_ref, k_ref, v_ref, rg_ref, la_ref):
    dk = wq_ref.shape[1] // B_HEADS
    xn = _rms(h_ref[...], g_ref[...]).astype(BF16)
    q_ref[...] = (_dot(xn, wq_ref[...]) * (dk ** -0.5)).astype(BF16)
    k_ref[...] = _dot(xn, wk_ref[...]).astype(BF16)
    v_ref[...] = _dot(xn, wv_ref[...]).astype(BF16)
    rg_ref[...] = jax.nn.silu(_dot(xn, wr_ref[...])).astype(BF16)
    g_low = _dot(xn, wg1_ref[...]).astype(BF16)
    la_ref[...] = jax.nn.log_sigmoid(_dot(g_low, wg2_ref[...]) + gb_ref[...]) / B_TAU


def _bcast_sublane(x3, idx):
    return jnp.broadcast_to(x3[:, idx:idx + 1, :], x3.shape)


def _boundary_rows(b, half):
    L, w = b.shape
    if half >= SUBLANES:
        nb = L // (2 * half)
        b4 = b.reshape(nb, 2 * half // SUBLANES, SUBLANES, w)
        last = b4[:, half // SUBLANES - 1, SUBLANES - 1:SUBLANES, :]
        return jnp.broadcast_to(last[:, None, :, :], b4.shape).reshape(L, w)
    b3 = b.reshape(L // SUBLANES, SUBLANES, w)
    sub = lax.broadcasted_iota(jnp.int32, b3.shape, 1)
    out = None
    for blk in range(SUBLANES // (2 * half)):
        cand = _bcast_sublane(b3, blk * 2 * half + half - 1)
        out = cand if out is None else jnp.where(sub >= blk * 2 * half, cand, out)
    return out.reshape(L, w)


def _gla_scan_kernel(q_ref, k_ref, v_ref, rg_ref, la_ref, ng_ref, wout_ref, h_ref, o_ref,
                     st_ref, y_ref):
    ts = q_ref.shape[0]
    dk = q_ref.shape[1] // B_HEADS
    dv = v_ref.shape[1] // B_HEADS
    L = GLA_CHUNK

    @pl.when(pl.program_id(1) == 0)
    def _():
        st_ref[...] = jnp.zeros(st_ref.shape, F32)

    row = lax.broadcasted_iota(jnp.int32, (L, L), 0)
    col = lax.broadcasted_iota(jnp.int32, (L, L), 1)
    tri = (col <= row).astype(BF16)
    eye = (col == row).astype(F32)
    eye_k = (lax.broadcasted_iota(jnp.int32, (dk, dk), 0) == lax.broadcasted_iota(jnp.int32, (dk, dk), 1))

    levels = []
    half = L // 2
    while half >= 1:
        pairs = ((row // (2 * half)) == (col // (2 * half))) & ((row & half) != 0) & ((col & half) == 0)
        levels.append((half, pairs.astype(F32)))
        half //= 2

    for c in range(ts // L):
        rows = slice(c * L, (c + 1) * L)
        b_all = _dot_exact_lhs01(tri, la_ref[rows, :])
        for hd in range(B_HEADS):
            kc = slice(hd * dk, (hd + 1) * dk)
            vc = slice(hd * dv, (hd + 1) * dv)
            b = b_all[:, kc]
            qh = q_ref[rows, kc].astype(F32)
            kh = k_ref[rows, kc].astype(F32)
            vh = v_ref[rows, vc]
            s_mat = eye * _dot_nt(qh.astype(BF16), kh.astype(BF16))
            for half, pairs in levels:
                e = jnp.exp(-jnp.abs(b - _boundary_rows(b, half)))
                s_mat = s_mat + pairs * _dot_nt((qh * e).astype(BF16), (kh * e).astype(BF16))
            state = st_ref[hd]
            o = _dot(s_mat.astype(BF16), vh) + _dot((qh * jnp.exp(b)).astype(BF16), state.astype(BF16))
            b_last = b[L - 1:L, :]
            k_dec = kh * jnp.exp(b_last - b)
            decay_col = jnp.sum(jnp.where(eye_k, jnp.exp(b_last), 0.0), axis=1, keepdims=True)
            st_ref[hd] = decay_col * state + _dot_tn(k_dec.astype(BF16), vh)
            yh = _head_rms(o, ng_ref[:, vc]) * rg_ref[rows, vc].astype(F32)
            y_ref[rows, vc] = yh.astype(BF16)

    o_ref[...] = h_ref[...] + _dot(y_ref[...], wout_ref[...])


def _gla_layer(h, gain, w_in, gate_w2, gate_b, norm_g, w_out, batch):
    rows, d = h.shape
    seq = rows // batch
    nk = gate_w2.shape[1]
    dk = nk // B_HEADS
    dv = d // B_HEADS
    nv = B_HEADS * dv
    wq = w_in[:, :nk].astype(BF16)
    wk = w_in[:, nk:2 * nk].astype(BF16)
    wv = w_in[:, 2 * nk:2 * nk + nv].astype(BF16)
    wr = w_in[:, 2 * nk + nv:2 * nk + 2 * nv].astype(BF16)
    wg1 = jnp.pad(w_in[:, 2 * nk + 2 * nv:], ((0, 0), (0, LANES - B_RANK))).astype(BF16)
    wg2 = jnp.pad(gate_w2, ((0, LANES - B_RANK), (0, 0))).astype(BF16)
    tm = ROW_TILE
    row_tile = lambda i: (i, 0)
    q, k, v, rg, la = pl.pallas_call(
        _gla_in_kernel,
        grid=(rows // tm,),
        in_specs=[pl.BlockSpec((tm, d), row_tile), _whole((1, d)), _whole((d, nk)), _whole((d, nk)),
                  _whole((d, nv)), _whole((d, nv)), _whole((d, LANES)), _whole((LANES, nk)),
                  _whole((1, nk))],
        out_specs=[pl.BlockSpec((tm, nk), row_tile), pl.BlockSpec((tm, nk), row_tile),
                   pl.BlockSpec((tm, nv), row_tile), pl.BlockSpec((tm, nv), row_tile),
                   pl.BlockSpec((tm, nk), row_tile)],
        out_shape=[jax.ShapeDtypeStruct((rows, nk), BF16), jax.ShapeDtypeStruct((rows, nk), BF16),
                   jax.ShapeDtypeStruct((rows, nv), BF16), jax.ShapeDtypeStruct((rows, nv), BF16),
                   jax.ShapeDtypeStruct((rows, nk), F32)],
        compiler_params=_params(("parallel",)),
        name="gla_in",
    )(h, gain, wq, wk, wv, wr, wg1, wg2, gate_b.reshape(1, nk))

    ts = GLA_TILE
    per_batch = seq // ts
    tile = lambda b, i: (b * per_batch + i, 0)
    return pl.pallas_call(
        _gla_scan_kernel,
        grid=(batch, per_batch),
        in_specs=[pl.BlockSpec((ts, nk), tile), pl.BlockSpec((ts, nk), tile),
                  pl.BlockSpec((ts, nv), tile), pl.BlockSpec((ts, nv), tile),
                  pl.BlockSpec((ts, nk), tile), _whole((1, nv)), _whole((nv, d)),
                  pl.BlockSpec((ts, d), tile)],
        out_specs=pl.BlockSpec((ts, d), tile),
        out_shape=jax.ShapeDtypeStruct((rows, d), F32),
        scratch_shapes=[pltpu.VMEM((B_HEADS, dk, dv), F32), pltpu.VMEM((ts, nv), BF16)],
        compiler_params=_params(("parallel", "arbitrary")),
        name="gla_scan",
    )(q, k, v, rg, la, norm_g.reshape(1, nv), w_out.astype(BF16), h)


def _s5_kernel(h_ref, g_ref, win_ref, perm_ref, permt_ref, bmat_ref, acoef_ref, apow_ref, atile_ref,
               cmat_ref, dskip_ref, wg_ref, bg_ref, wo_ref, o_ref, carry_ref, bu_ref, s_ref, sb_ref, yp_ref):
    ts, d = h_ref.shape
    nblk, bw = bmat_ref.shape[0], bmat_ref.shape[1]
    half = bmat_ref.shape[2] // 2
    nj = ts // SUBLANES

    @pl.when(pl.program_id(1) == 0)
    def _():
        carry_ref[...] = jnp.zeros(carry_ref.shape, F32)

    xn = _rms(h_ref[...], g_ref[...]).astype(BF16)
    xp = _dot(perm_ref[...], xn).astype(BF16)
    u = _dot(xp, win_ref[...])
    ub = u.astype(BF16)

    for blk in range(nblk):
        bu_ref[...] = _dot(ub[:, blk * bw:(blk + 1) * bw], bmat_ref[blk])
        a_re = acoef_ref[blk, 0]
        a_im = acoef_ref[blk, 1]

        def local_step(j, x):
            x_re, x_im = x
            r = pl.ds(pl.multiple_of(j * SUBLANES, SUBLANES), SUBLANES)
            n_re = a_re * x_re - a_im * x_im + bu_ref[r, 0:half]
            n_im = a_re * x_im + a_im * x_re + bu_ref[r, half:2 * half]
            s_ref[r, 0:half] = n_re
            s_ref[r, half:2 * half] = n_im
            return n_re, n_im

        zero = jnp.zeros((SUBLANES, half), F32)
        end_re, end_im = lax.fori_loop(0, nj, local_step, (zero, zero), unroll=S5_UNROLL)

        t_re = atile_ref[blk, 0]
        t_im = atile_ref[blk, 1]
        c_re = carry_ref[blk, 0:1, 0:half]
        c_im = carry_ref[blk, 0:1, half:2 * half]
        cin_re, cin_im = [], []
        for r in range(SUBLANES):
            cin_re.append(c_re)
            cin_im.append(c_im)
            e_re = end_re[r:r + 1, :] + t_re * c_re - t_im * c_im
            e_im = end_im[r:r + 1, :] + t_re * c_im + t_im * c_re
            c_re, c_im = e_re, e_im
        carry_ref[blk, :, 0:half] = jnp.broadcast_to(c_re, (SUBLANES, half))
        carry_ref[blk, :, half:2 * half] = jnp.broadcast_to(c_im, (SUBLANES, half))
        cin_re = jnp.concatenate(cin_re, axis=0)
        cin_im = jnp.concatenate(cin_im, axis=0)

        def fix_step(jj, _):
            out_re, out_im = [], []
            for k in range(2):
                j = 2 * jj + k
                r = pl.ds(pl.multiple_of(j * SUBLANES, SUBLANES), SUBLANES)
                p_re = apow_ref[blk, 0, pl.ds(j, 1), :]
                p_im = apow_ref[blk, 1, pl.ds(j, 1), :]
                out_re.append(s_ref[r, 0:half] + p_re * cin_re - p_im * cin_im)
                out_im.append(s_ref[r, half:2 * half] + p_re * cin_im + p_im * cin_re)
            r2 = pl.ds(pl.multiple_of(jj * 2 * SUBLANES, 2 * SUBLANES), 2 * SUBLANES)
            sb_ref[r2, 0:half] = jnp.concatenate(out_re, axis=0).astype(BF16)
            sb_ref[r2, half:2 * half] = jnp.concatenate(out_im, axis=0).astype(BF16)
            return 0

        lax.fori_loop(0, nj // 2, fix_step, 0, unroll=S5_UNROLL // 2)
        yp_ref[:, blk * bw:(blk + 1) * bw] = _dot(sb_ref[...], cmat_ref[blk])

    y = jax.nn.gelu(yp_ref[...] + dskip_ref[...] * u)
    z = y * jax.nn.sigmoid(_dot(y.astype(BF16), wg_ref[...]) + bg_ref[...])
    z = _dot(permt_ref[...], z.astype(BF16)).astype(BF16)
    o_ref[...] = h_ref[...] + _dot(z, wo_ref[...])


def _s5_discretise(lam_re, lam_im, log_dt, b_re, b_im, c_re, c_im, ts):
    groups, states = lam_re.shape
    gpb = S5_BLOCK_GROUPS
    nblk = groups // gpb
    lr = lam_re.astype(F32)
    li = lam_im.astype(F32)
    dt = jnp.exp(log_dt.astype(F32))[:, None]
    mag = jnp.exp(lr * dt)
    ab_re = mag * jnp.cos(li * dt)
    ab_im = mag * jnp.sin(li * dt)
    den = lr * lr + li * li
    z_re = ((ab_re - 1.0) * lr + ab_im * li) / den
    z_im = (ab_im * lr - (ab_re - 1.0) * li) / den
    br = b_re.astype(F32)
    bi = b_im.astype(F32)
    bb_re = z_re[..., None] * br - z_im[..., None] * bi
    bb_im = z_re[..., None] * bi + z_im[..., None] * br
    eye = jnp.eye(gpb, dtype=F32)

    def in_block(bb):
        t = bb.reshape(nblk, gpb, states, C_GROUP)
        return jnp.einsum('ngph,gk->nghkp', t, eye).reshape(nblk, gpb * C_GROUP, gpb * states)

    bmat = jnp.concatenate([in_block(bb_re), in_block(bb_im)], axis=-1).astype(BF16)

    def out_block(cc):
        t = cc.astype(F32).reshape(nblk, gpb, C_GROUP, states)
        return jnp.einsum('nghp,gk->ngpkh', t, eye).reshape(nblk, gpb * states, gpb * C_GROUP)

    cmat = jnp.concatenate([out_block(c_re), -out_block(c_im)], axis=1).astype(BF16)

    def powers(n):
        n = n.astype(F32)[:, None, None]
        m = jnp.exp(n * (lr * dt)[None])
        ang = n * (li * dt)[None]
        pr = (m * jnp.cos(ang)).reshape(-1, nblk, gpb * states).transpose(1, 0, 2)
        pi = (m * jnp.sin(ang)).reshape(-1, nblk, gpb * states).transpose(1, 0, 2)
        return jnp.stack([pr, pi], axis=1)

    nj = ts // SUBLANES
    acoef = jnp.broadcast_to(powers(jnp.ones((1,)))[:, :, 0:1, :], (nblk, 2, SUBLANES, gpb * states))
    apow = powers(jnp.arange(1, nj + 1))
    atile = powers(jnp.full((1,), nj))
    return bmat, cmat, acoef, apow, atile


def _s5_layer(h, gain, w_in, lam_re, lam_im, log_dt, b_re, b_im, c_re, c_im, d_skip, w_gate, b_gate,
              w_out, batch):
    rows, d = h.shape
    seq = rows // batch
    ts = S5_TILE
    nj = ts // SUBLANES
    per_batch = seq // ts
    bmat, cmat, acoef, apow, atile = _s5_discretise(lam_re, lam_im, log_dt, b_re, b_im, c_re, c_im, ts)
    nblk, _, sw = bmat.shape
    p = jnp.arange(ts)
    perm = (p[None, :] == ((p % SUBLANES) * nj + p // SUBLANES)[:, None]).astype(BF16)
    tile = lambda b, i: (b * per_batch + i, 0)
    return pl.pallas_call(
        _s5_kernel,
        grid=(batch, per_batch),
        in_specs=[pl.BlockSpec((ts, d), tile), _whole((1, d)), _whole((d, d)), _whole((ts, ts)), _whole((ts, ts)),
                  _whole(bmat.shape), _whole(acoef.shape), _whole(apow.shape), _whole(atile.shape),
                  _whole(cmat.shape), _whole((1, d)), _whole((d, d)), _whole((1, d)), _whole((d, d))],
        out_specs=pl.BlockSpec((ts, d), tile),
        out_shape=jax.ShapeDtypeStruct((rows, d), F32),
        scratch_shapes=[pltpu.VMEM((nblk, SUBLANES, sw), F32), pltpu.VMEM((ts, sw), F32),
                        pltpu.VMEM((ts, sw), F32), pltpu.VMEM((ts, sw), BF16),
                        pltpu.VMEM((ts, d), F32)],
        compiler_params=_params(("parallel", "arbitrary")),
        name="s5_mix",
    )(h, gain, w_in.astype(BF16), perm, perm.T, bmat, acoef, apow, atile, cmat, d_skip.reshape(1, d),
      w_gate.astype(BF16), b_gate.reshape(1, d), w_out.astype(BF16))


def kernel(x, mem, norm_mix, norm_x, norm_ffn, norm_mem, norm_final, a_w_in, a_conv_w, a_gate_b, a_norm, a_w_out, b_w_in, b_gate_w2, b_gate_b, b_norm, b_w_out, c_w_in, c_lam_re, c_lam_im, c_log_dt, c_b_re, c_b_im, c_c_re, c_c_im, c_d, c_w_gate, c_b_gate, c_w_out, x_w_q, x_w_kv, x_w_o, f_w_gate, f_w_up, f_w_down):
    batch, seq, d = x.shape
    mem_len = mem.shape[1]
    depth = norm_mix.shape[0]
    assert seq % ROW_TILE == 0 and seq % MLSTM_TILE == 0 and seq % GLA_TILE == 0 and seq % S5_TILE == 0
    assert MLSTM_TILE % MIX_CHUNK == 0 and GLA_TILE % GLA_CHUNK == 0 and seq % MLSTM_IN_TILE == 0
    assert f_w_gate.shape[2] % FF_CHUNK == 0 and d % (X_HEADS * LANES) == 0 and (batch * seq) % FF_TILE == 0

    kv = _mem_kv(mem.reshape(batch * mem_len, d), norm_mem.reshape(1, d), x_w_kv.astype(BF16), mem_len)

    h = x.reshape(batch * seq, d)
    for i in range(depth):
        kind = i % N_MIXERS
        j = i // N_MIXERS
        gain = norm_mix[i].reshape(1, d)
        if kind == 0:
            h = _mlstm_layer(h, gain, a_w_in[j], a_conv_w[j], a_gate_b[j], a_norm[j], a_w_out[j], batch)
        elif kind == 1:
            h = _gla_layer(h, gain, b_w_in[j], b_gate_w2[j], b_gate_b[j], b_norm[j], b_w_out[j], batch)
        else:
            h = _s5_layer(h, gain, c_w_in[j], c_lam_re[j], c_lam_im[j], c_log_dt[j], c_b_re[j], c_b_im[j],
                          c_c_re[j], c_c_im[j], c_d[j], c_w_gate[j], c_b_gate[j], c_w_out[j], batch)
        h = _xattn(h, norm_x[i].reshape(1, d), x_w_q[i].astype(BF16), kv, i, x_w_o[i].astype(BF16),
                   batch, mem_len)
        h = _swiglu(h, norm_ffn[i].reshape(1, d), f_w_gate[i].astype(BF16), f_w_up[i].astype(BF16),
                    f_w_down[i].astype(BF16), norm_final.reshape(1, d), final_norm=(i == depth - 1))
    return h.reshape(batch, seq, d)
```

```python
import functools
import math

import jax
import jax.numpy as jnp
from jax import lax
from jax.experimental import pallas as pl
from jax.experimental.pallas import tpu as pltpu

F32 = jnp.float32
BF16 = jnp.bfloat16
EPS = 1e-6

N_MIXERS = 3
A_HEADS = 4
A_CONV = 4
B_HEADS = 4
B_RANK = 16
B_TAU = 16.0
C_GROUP = 16
C_STATE = 64
X_HEADS = 4

LANES = 128
SUBLANES = 8
VMEM_LIMIT_BYTES = 56 * 1024 * 1024

ROW_TILE = 1024
FF_TILE = 1024
SUB_TILE = 256
MLSTM_IN_TILE = 1024
MLSTM_TILE = 1024
GLA_TILE = 1024
MIX_CHUNK = 512
GLA_CHUNK = 128
S5_TILE = 512
S5_UNROLL = S5_TILE // SUBLANES
FF_CHUNK = 256
S5_BLOCK_GROUPS = 2 * LANES // C_GROUP


def _params(semantics):
    return pltpu.CompilerParams(dimension_semantics=semantics, vmem_limit_bytes=VMEM_LIMIT_BYTES)


def _whole(shape, single=False):
    zeros = (0,) * len(shape)
    if single:
        return pl.BlockSpec(shape, lambda *_: zeros, pipeline_mode=pl.Buffered(1))
    return pl.BlockSpec(shape, lambda *_: zeros)


def _rms(x, g):
    return x * lax.rsqrt(jnp.mean(x * x, axis=-1, keepdims=True) + EPS) * g


def _dot(a, b):
    return jnp.dot(a, b, preferred_element_type=F32)


def _dot_nt(a, b):
    return lax.dot_general(a, b, (((1,), (1,)), ((), ())), preferred_element_type=F32)


def _dot_tn(a, b):
    return lax.dot_general(a, b, (((0,), (0,)), ((), ())), preferred_element_type=F32)


def _split3(x):
    hi = x.astype(BF16)
    r = x - hi.astype(F32)
    mid = r.astype(BF16)
    lo = (r - mid.astype(F32)).astype(BF16)
    return hi, mid, lo


def _dot_exact_lhs01(m01, x):
    hi, mid, lo = _split3(x)
    return _dot(m01, hi) + _dot(m01, mid) + _dot(m01, lo)


def _head_rms(o, g):
    return o * lax.rsqrt(jnp.mean(o * o, axis=-1, keepdims=True) + EPS) * g


def _mem_kv_kernel(x_ref, g_ref, w_ref, o_ref):
    xn = _rms(x_ref[...], g_ref[...]).astype(BF16)
    o_ref[...] = _dot(xn, w_ref[0]).astype(BF16)


def _mem_kv(x, g, w, tm):
    rows, d = x.shape
    depth, _, n = w.shape
    return pl.pallas_call(
        _mem_kv_kernel,
        grid=(depth, rows // tm),
        in_specs=[pl.BlockSpec((tm, d), lambda l, i: (i, 0)), _whole((1, d)),
                  pl.BlockSpec((1, d, n), lambda l, i: (l, 0, 0))],
        out_specs=pl.BlockSpec((tm, n), lambda l, i: (i, l)),
        out_shape=jax.ShapeDtypeStruct((rows, depth * n), BF16),
        compiler_params=_params(("parallel", "parallel")),
        name="mem_kv_proj",
    )(x, g, w)


def _xattn_kernel(h_ref, g_ref, wq_ref, kv_ref, wo_ref, o_ref):
    d = h_ref.shape[1]
    dh = d // X_HEADS
    x = h_ref[...]
    xn = _rms(x, g_ref[...]).astype(BF16)
    q = _dot(xn, wq_ref[...]).astype(BF16)
    outs = []
    for hd in range(X_HEADS):
        qh = q[:, hd * dh:(hd + 1) * dh]
        kh = kv_ref[:, hd * dh:(hd + 1) * dh]
        vh = kv_ref[:, d + hd * dh:d + (hd + 1) * dh]
        s = _dot_nt(qh, kh) * (dh ** -0.5)
        e = jnp.exp(s - jnp.max(s, axis=-1, keepdims=True))
        p = e / jnp.sum(e, axis=-1, keepdims=True)
        outs.append(_dot(p.astype(BF16), vh))
    o = jnp.concatenate(outs, axis=-1).astype(BF16)
    o_ref[...] = x + _dot(o, wo_ref[...])


def _xattn(h, g, wq, kv, layer, wo, batch, mem_len):
    rows, d = h.shape
    tm = ROW_TILE
    per_batch = rows // batch // tm
    return pl.pallas_call(
        _xattn_kernel,
        grid=(batch, per_batch),
        in_specs=[
            pl.BlockSpec((tm, d), lambda b, i: (b * per_batch + i, 0)),
            _whole((1, d)),
            _whole((d, d)),
            pl.BlockSpec((mem_len, 2 * d), lambda b, i: (b, layer)),
            _whole((d, d)),
        ],
        out_specs=pl.BlockSpec((tm, d), lambda b, i: (b * per_batch + i, 0)),
        out_shape=jax.ShapeDtypeStruct((rows, d), F32),
        compiler_params=_params(("parallel", "parallel")),
        name="xattn",
    )(h, g, wq, kv, wo)


def _swiglu_kernel(h_ref, g_ref, wg_ref, wu_ref, wd_ref, gf_ref, o_ref, *, final_norm):
    x = h_ref[...]
    xn = _rms(x, g_ref[...]).astype(BF16)
    o_ref[...] = x
    for c in range(wg_ref.shape[1] // FF_CHUNK):
        cols = slice(c * FF_CHUNK, (c + 1) * FF_CHUNK)
        gate = _dot(xn, wg_ref[:, cols])
        up = _dot(xn, wu_ref[:, cols])
        act = (jax.nn.silu(gate) * up).astype(BF16)
        o_ref[...] += _dot(act, wd_ref[cols, :])
    if final_norm:
        o_ref[...] = _rms(o_ref[...], gf_ref[...])


def _swiglu(h, g, wg, wu, wd, gf, final_norm):
    rows, d = h.shape
    tm = FF_TILE
    return pl.pallas_call(
        functools.partial(_swiglu_kernel, final_norm=final_norm),
        grid=(rows // tm,),
        in_specs=[pl.BlockSpec((tm, d), lambda i: (i, 0)), _whole((1, d)), _whole(wg.shape, single=True),
                  _whole(wu.shape, single=True), _whole(wd.shape, single=True), _whole((1, d))],
        out_specs=pl.BlockSpec((tm, d), lambda i: (i, 0)),
        out_shape=jax.ShapeDtypeStruct((rows, d), F32),
        compiler_params=_params(("parallel",)),
        name="swiglu",
    )(h, g, wg, wu, wd, gf)


def _mlstm_project(h_ref, g_ref, wqk_ref, wv_ref, wo_ref, wgt_ref, cw_ref, gb_ref,
                   q_ref, k_ref, v_ref, og_ref, gr_ref, xp_ref):
    tm, d = h_ref.shape
    nqk = wqk_ref.shape[1]
    dk = nqk // (2 * A_HEADS)
    cw = cw_ref[...]
    xn_all = _rms(h_ref[...], g_ref[...]).astype(BF16)

    gr = _dot_nt(wgt_ref[...], xn_all) + gb_ref[...]
    is_input_gate = lax.broadcasted_iota(jnp.int32, gr.shape, 0) < A_HEADS
    gr = jnp.where(is_input_gate, gr, jax.nn.log_sigmoid(gr))
    pos = lax.broadcasted_iota(jnp.int32, gr.shape, 1) % MIX_CHUNK

    def chunk_scan(x, op):
        k = 1
        while k < MIX_CHUNK:
            x = jnp.where(pos >= k, op(x, pltpu.roll(x, k, axis=1)), x)
            k *= 2
        return x

    b = chunk_scan(gr, jnp.add)
    beta = pltpu.roll(gr, A_HEADS, axis=0) - b
    ng = gr.shape[0]
    gr_ref[0:ng, :] = beta
    gr_ref[ng:2 * ng, :] = b
    gr_ref[2 * ng:3 * ng, :] = chunk_scan(beta, jnp.maximum)

    for r0 in range(0, tm, SUB_TILE):
        rows = slice(r0, r0 + SUB_TILE)
        xn = xn_all[rows, :]
        qk = _dot(xn, wqk_ref[...])
        xp_ref[SUBLANES + r0:SUBLANES + r0 + SUB_TILE, :] = qk
        y = cw[A_CONV - 1:A_CONV, :] * qk
        for j in range(A_CONV - 1):
            y = y + cw[j:j + 1, :] * xp_ref[pl.ds(r0 + SUBLANES - (A_CONV - 1) + j, SUB_TILE), :]
        y = jax.nn.silu(y)
        q_ref[rows, :] = (y[:, :nqk // 2] * (dk ** -0.5)).astype(BF16)
        k_ref[rows, :] = y[:, nqk // 2:].astype(BF16)
        v_ref[rows, :] = _dot(xn, wv_ref[...]).astype(BF16)
        og_ref[rows, :] = jax.nn.sigmoid(_dot(xn, wo_ref[...])).astype(BF16)
    xp_ref[0:SUBLANES, :] = xp_ref[tm:tm + SUBLANES, :]


def _mlstm_recur(q_ref, k_ref, v_ref, og_ref, gr_ref, ng_ref, wout_ref, h_ref, o_ref,
                 cn_ref, m_ref, y_ref):
    ts = q_ref.shape[0]
    dk = q_ref.shape[1] // A_HEADS
    dv = v_ref.shape[1] // A_HEADS
    L = MIX_CHUNK
    H = A_HEADS
    causal = lax.broadcasted_iota(jnp.int32, (L, L), 1) <= lax.broadcasted_iota(jnp.int32, (L, L), 0)
    ones = jnp.ones((L, LANES), BF16)

    for c in range(ts // L):
        rows = slice(c * L, (c + 1) * L)
        beta = gr_ref[0:2 * H, rows]
        b_rows = gr_ref[2 * H:4 * H, rows]
        m_prev = m_ref[:, 0:1]
        big_m = jnp.maximum(m_prev, gr_ref[4 * H:6 * H, rows])
        m_last = big_m[:, L - 1:L]
        decay_all = jnp.exp(m_prev - m_last)
        m_ref[...] = jnp.broadcast_to(b_rows[:, L - 1:L] + m_last, m_ref.shape)
        m_cols = big_m.T
        w_inter_cols = jnp.exp(m_prev - big_m).T
        exp_neg_m_cols = jnp.exp(-(b_rows + big_m)).T
        wa_cols = jnp.exp(beta - m_last).T
        for hd in range(A_HEADS):
            j = H + hd
            qh = q_ref[rows, hd * dk:(hd + 1) * dk]
            kh = k_ref[rows, hd * dk:(hd + 1) * dk]
            v1 = jnp.concatenate([v_ref[rows, hd * dv:(hd + 1) * dv], ones], axis=1)
            cn_prev = cn_ref[hd]
            w_intra = jnp.exp(jnp.where(causal, beta[j:j + 1, :] - m_cols[:, j:j + 1], -jnp.inf))
            w_inter = w_inter_cols[:, j:j + 1]
            s = (_dot_nt(qh, kh) * w_intra).astype(BF16)
            both = _dot(s, v1) + w_inter * _dot(qh, cn_prev.astype(BF16))
            inv = 1.0 / jnp.maximum(jnp.abs(both[:, dv:]), exp_neg_m_cols[:, j:j + 1])
            hh = both[:, :dv] * jnp.concatenate([inv] * (dv // LANES), axis=1)
            kw = (kh.astype(F32) * wa_cols[:, j:j + 1]).astype(BF16)
            cn_ref[hd] = decay_all[j:j + 1, :] * cn_prev + _dot_tn(kw, v1)
            cols = slice(hd * dv, (hd + 1) * dv)
            yh = _head_rms(hh, ng_ref[:, cols]) * og_ref[rows, cols].astype(F32)
            y_ref[rows, cols] = yh.astype(BF16)

    o_ref[...] = h_ref[...] + _dot(y_ref[...], wout_ref[...])


def _mlstm_in_kernel(h_ref, g_ref, wqk_ref, wv_ref, wo_ref, wgt_ref, cw_ref, gb_ref,
                     q_ref, k_ref, v_ref, og_ref, gr_ref, xp_ref, *, tiles_per_seq):
    @pl.when(pl.program_id(0) % tiles_per_seq == 0)
    def _():
        xp_ref[0:SUBLANES, :] = jnp.zeros((SUBLANES, xp_ref.shape[1]), F32)

    _mlstm_project(h_ref, g_ref, wqk_ref, wv_ref, wo_ref, wgt_ref, cw_ref, gb_ref,
                   q_ref, k_ref, v_ref, og_ref, gr_ref, xp_ref)


def _mlstm_scan_kernel(q_ref, k_ref, v_ref, og_ref, gr_ref, ng_ref, wout_ref, h_ref, o_ref,
                       cn_ref, m_ref, y_ref):
    @pl.when(pl.program_id(1) == 0)
    def _():
        cn_ref[...] = jnp.zeros(cn_ref.shape, F32)
        m_ref[...] = jnp.zeros(m_ref.shape, F32)

    _mlstm_recur(q_ref, k_ref, v_ref, og_ref, gr_ref, ng_ref, wout_ref, h_ref, o_ref,
                 cn_ref, m_ref, y_ref)


def _mlstm_layer(h, gain, w_in, conv_w, gate_b, norm_g, w_out, batch):
    rows, d = h.shape
    seq = rows // batch
    nqk = conv_w.shape[1]
    dk = nqk // (2 * A_HEADS)
    dv = d // A_HEADS
    nv = A_HEADS * dv
    ng = 2 * A_HEADS
    wqk = w_in[:, :nqk].astype(BF16)
    wv = w_in[:, nqk:nqk + nv].astype(BF16)
    wo = w_in[:, nqk + nv:nqk + 2 * nv].astype(BF16)
    wgt = w_in[:, nqk + 2 * nv:].T.astype(BF16)
    tm = MLSTM_IN_TILE
    q, k, v, og, gr = pl.pallas_call(
        functools.partial(_mlstm_in_kernel, tiles_per_seq=seq // tm),
        grid=(rows // tm,),
        in_specs=[pl.BlockSpec((tm, d), lambda i: (i, 0)), _whole((1, d)), _whole((d, nqk)),
                  _whole((d, nv)), _whole((d, nv)), _whole((ng, d)), _whole((A_CONV, nqk)),
                  _whole((ng, 1))],
        out_specs=[pl.BlockSpec((tm, nqk // 2), lambda i: (i, 0)),
                   pl.BlockSpec((tm, nqk // 2), lambda i: (i, 0)),
                   pl.BlockSpec((tm, nv), lambda i: (i, 0)),
                   pl.BlockSpec((tm, nv), lambda i: (i, 0)),
                   pl.BlockSpec((3 * ng, tm), lambda i: (0, i))],
        out_shape=[jax.ShapeDtypeStruct((rows, nqk // 2), BF16),
                   jax.ShapeDtypeStruct((rows, nqk // 2), BF16),
                   jax.ShapeDtypeStruct((rows, nv), BF16),
                   jax.ShapeDtypeStruct((rows, nv), BF16),
                   jax.ShapeDtypeStruct((3 * ng, rows), F32)],
        scratch_shapes=[pltpu.VMEM((tm + SUBLANES, nqk), F32)],
        compiler_params=_params(("arbitrary",)),
        name="mlstm_in",
    )(h, gain, wqk, wv, wo, wgt, conv_w, gate_b.reshape(ng, 1))

    ts = MLSTM_TILE
    per_batch = seq // ts
    tile = lambda b, i: (b * per_batch + i, 0)
    return pl.pallas_call(
        _mlstm_scan_kernel,
        grid=(batch, per_batch),
        in_specs=[pl.BlockSpec((ts, nqk // 2), tile), pl.BlockSpec((ts, nqk // 2), tile),
                  pl.BlockSpec((ts, nv), tile), pl.BlockSpec((ts, nv), tile),
                  pl.BlockSpec((3 * ng, ts), lambda b, i: (0, b * per_batch + i)),
                  _whole((1, nv)), _whole((nv, d)), pl.BlockSpec((ts, d), tile)],
        out_specs=pl.BlockSpec((ts, d), tile),
        out_shape=jax.ShapeDtypeStruct((rows, d), F32),
        scratch_shapes=[pltpu.VMEM((A_HEADS, dk, dv + LANES), F32),
                        pltpu.VMEM((ng, LANES), F32),
                        pltpu.VMEM((ts, nv), BF16)],
        compiler_params=_params(("parallel", "arbitrary")),
        name="mlstm_scan",
    )(q, k, v, og, gr, norm_g.reshape(1, nv), w_out.astype(BF16), h)


def _gla_in_kernel(h_ref, g_ref, wq_ref, wk_ref, wv_ref, wr_ref, wg1_ref, wg2_ref, gb_ref,
                   q_ref, k_ref, v_ref, rg_ref, la_ref):
    dk = wq_ref.shape[1] // B_HEADS
    xn = _rms(h_ref[...], g_ref[...]).astype(BF16)
    q_ref[...] = (_dot(xn, wq_ref[...]) * (dk ** -0.5)).astype(BF16)
    k_ref[...] = _dot(xn, wk_ref[...]).astype(BF16)
    v_ref[...] = _dot(xn, wv_ref[...]).astype(BF16)
    rg_ref[...] = jax.nn.silu(_dot(xn, wr_ref[...])).astype(BF16)
    g_low = _dot(xn, wg1_ref[...]).astype(BF16)
    log_a = jax.nn.log_sigmoid(_dot(g_low, wg2_ref[...]) + gb_ref[...]) / B_TAU
    L = GLA_CHUNK
    tri = (lax.broadcasted_iota(jnp.int32, (L, L), 1) <= lax.broadcasted_iota(jnp.int32, (L, L), 0)).astype(BF16)
    for r0 in range(0, h_ref.shape[0], L):
        la_ref[r0:r0 + L, :] = _dot_exact_lhs01(tri, log_a[r0:r0 + L, :])


def _bcast_sublane(x3, idx):
    return jnp.broadcast_to(x3[:, idx:idx + 1, :], x3.shape)


def _boundary_rows(b, half):
    L, w = b.shape
    if half >= SUBLANES:
        nb = L // (2 * half)
        b4 = b.reshape(nb, 2 * half // SUBLANES, SUBLANES, w)
        last = b4[:, half // SUBLANES - 1, SUBLANES - 1:SUBLANES, :]
        return jnp.broadcast_to(last[:, None, :, :], b4.shape).reshape(L, w)
    b3 = b.reshape(L // SUBLANES, SUBLANES, w)
    sub = lax.broadcasted_iota(jnp.int32, b3.shape, 1)
    out = None
    for blk in range(SUBLANES // (2 * half)):
        cand = _bcast_sublane(b3, blk * 2 * half + half - 1)
        out = cand if out is None else jnp.where(sub >= blk * 2 * half, cand, out)
    return out.reshape(L, w)


def _gla_scan_kernel(q_ref, k_ref, v_ref, rg_ref, la_ref, ng_ref, wout_ref, h_ref, o_ref,
                     st_ref, y_ref):
    ts = q_ref.shape[0]
    dk = q_ref.shape[1] // B_HEADS
    dv = v_ref.shape[1] // B_HEADS
    L = GLA_CHUNK

    @pl.when(pl.program_id(1) == 0)
    def _():
        st_ref[...] = jnp.zeros(st_ref.shape, F32)

    row = lax.broadcasted_iota(jnp.int32, (L, L), 0)
    col = lax.broadcasted_iota(jnp.int32, (L, L), 1)
    eye = (col == row).astype(F32)
    eye_k = (lax.broadcasted_iota(jnp.int32, (dk, dk), 0) == lax.broadcasted_iota(jnp.int32, (dk, dk), 1))

    levels = []
    half = L // 2
    while half >= 1:
        pairs = ((row // (2 * half)) == (col // (2 * half))) & ((row & half) != 0) & ((col & half) == 0)
        levels.append((half, pairs.astype(F32)))
        half //= 2

    for c in range(ts // L):
        rows = slice(c * L, (c + 1) * L)
        b_all = la_ref[rows, :]
        for hd in range(B_HEADS):
            kc = slice(hd * dk, (hd + 1) * dk)
            vc = slice(hd * dv, (hd + 1) * dv)
            b = b_all[:, kc]
            qh = q_ref[rows, kc].astype(F32)
            kh = k_ref[rows, kc].astype(F32)
            vh = v_ref[rows, vc]
            s_mat = eye * _dot_nt(qh.astype(BF16), kh.astype(BF16))
            for half, pairs in levels:
                e = jnp.exp(-jnp.abs(b - _boundary_rows(b, half)))
                s_mat = s_mat + pairs * _dot_nt((qh * e).astype(BF16), (kh * e).astype(BF16))
            state = st_ref[hd]
            o = _dot(s_mat.astype(BF16), vh) + _dot((qh * jnp.exp(b)).astype(BF16), state.astype(BF16))
            b_last = b[L - 1:L, :]
            k_dec = kh * jnp.exp(b_last - b)
            decay_col = jnp.sum(jnp.where(eye_k, jnp.exp(b_last), 0.0), axis=1, keepdims=True)
            st_ref[hd] = decay_col * state + _dot_tn(k_dec.astype(BF16), vh)
            yh = _head_rms(o, ng_ref[:, vc]) * rg_ref[rows, vc].astype(F32)
            y_ref[rows, vc] = yh.astype(BF16)

    o_ref[...] = h_ref[...] + _dot(y_ref[...], wout_ref[...])


def _gla_layer(h, gain, w_in, gate_w2, gate_b, norm_g, w_out, batch):
    rows, d = h.shape
    seq = rows // batch
    nk = gate_w2.shape[1]
    dk = nk // B_HEADS
    dv = d // B_HEADS
    nv = B_HEADS * dv
    wq = w_in[:, :nk].astype(BF16)
    wk = w_in[:, nk:2 * nk].astype(BF16)
    wv = w_in[:, 2 * nk:2 * nk + nv].astype(BF16)
    wr = w_in[:, 2 * nk + nv:2 * nk + 2 * nv].astype(BF16)
    wg1 = jnp.pad(w_in[:, 2 * nk + 2 * nv:], ((0, 0), (0, LANES - B_RANK))).astype(BF16)
    wg2 = jnp.pad(gate_w2, ((0, LANES - B_RANK), (0, 0))).astype(BF16)
    tm = ROW_TILE
    row_tile = lambda i: (i, 0)
    q, k, v, rg, la = pl.pallas_call(
        _gla_in_kernel,
        grid=(rows // tm,),
        in_specs=[pl.BlockSpec((tm, d), row_tile), _whole((1, d)), _whole((d, nk)), _whole((d, nk)),
                  _whole((d, nv)), _whole((d, nv)), _whole((d, LANES)), _whole((LANES, nk)),
                  _whole((1, nk))],
        out_specs=[pl.BlockSpec((tm, nk), row_tile), pl.BlockSpec((tm, nk), row_tile),
                   pl.BlockSpec((tm, nv), row_tile), pl.BlockSpec((tm, nv), row_tile),
                   pl.BlockSpec((tm, nk), row_tile)],
        out_shape=[jax.ShapeDtypeStruct((rows, nk), BF16), jax.ShapeDtypeStruct((rows, nk), BF16),
                   jax.ShapeDtypeStruct((rows, nv), BF16), jax.ShapeDtypeStruct((rows, nv), BF16),
                   jax.ShapeDtypeStruct((rows, nk), F32)],
        compiler_params=_params(("parallel",)),
        name="gla_in",
    )(h, gain, wq, wk, wv, wr, wg1, wg2, gate_b.reshape(1, nk))

    ts = GLA_TILE
    per_batch = seq // ts
    tile = lambda b, i: (b * per_batch + i, 0)
    return pl.pallas_call(
        _gla_scan_kernel,
        grid=(batch, per_batch),
        in_specs=[pl.BlockSpec((ts, nk), tile), pl.BlockSpec((ts, nk), tile),
                  pl.BlockSpec((ts, nv), tile), pl.BlockSpec((ts, nv), tile),
                  pl.BlockSpec((ts, nk), tile), _whole((1, nv)), _whole((nv, d)),
                  pl.BlockSpec((ts, d), tile)],
        out_specs=pl.BlockSpec((ts, d), tile),
        out_shape=jax.ShapeDtypeStruct((rows, d), F32),
        scratch_shapes=[pltpu.VMEM((B_HEADS, dk, dv), F32), pltpu.VMEM((ts, nv), BF16)],
        compiler_params=_params(("parallel", "arbitrary")),
        name="gla_scan",
    )(q, k, v, rg, la, norm_g.reshape(1, nv), w_out.astype(BF16), h)


def _s5_kernel(h_ref, g_ref, win_ref, perm_ref, permt_ref, bmat_ref, acoef_ref, apow_ref, atile_ref,
               cmat_ref, dskip_ref, wg_ref, bg_ref, wo_ref, o_ref, carry_ref, bu_ref, s_ref, sb_ref, yp_ref):
    ts, d = h_ref.shape
    nblk, bw = bmat_ref.shape[0], bmat_ref.shape[1]
    half = bmat_ref.shape[2] // 2
    nj = ts // SUBLANES

    @pl.when(pl.program_id(1) == 0)
    def _():
        carry_ref[...] = jnp.zeros(carry_ref.shape, F32)

    xn = _rms(h_ref[...], g_ref[...]).astype(BF16)
    xp = _dot(perm_ref[...], xn).astype(BF16)
    u = _dot(xp, win_ref[...])
    ub = u.astype(BF16)

    for blk in range(nblk):
        bu_ref[...] = _dot(ub[:, blk * bw:(blk + 1) * bw], bmat_ref[blk])
        a_re = acoef_ref[blk, 0]
        a_im = acoef_ref[blk, 1]

        def local_step(j, x):
            x_re, x_im = x
            r = pl.ds(pl.multiple_of(j * SUBLANES, SUBLANES), SUBLANES)
            n_re = a_re * x_re - a_im * x_im + bu_ref[r, 0:half]
            n_im = a_re * x_im + a_im * x_re + bu_ref[r, half:2 * half]
            s_ref[r, 0:half] = n_re
            s_ref[r, half:2 * half] = n_im
            return n_re, n_im

        zero = jnp.zeros((SUBLANES, half), F32)
        end_re, end_im = lax.fori_loop(0, nj, local_step, (zero, zero), unroll=S5_UNROLL)

        t_re = atile_ref[blk, 0]
        t_im = atile_ref[blk, 1]
        c_re = carry_ref[blk, 0:1, 0:half]
        c_im = carry_ref[blk, 0:1, half:2 * half]
        cin_re, cin_im = [], []
        for r in range(SUBLANES):
            cin_re.append(c_re)
            cin_im.append(c_im)
            e_re = end_re[r:r + 1, :] + t_re * c_re - t_im * c_im
            e_im = end_im[r:r + 1, :] + t_re * c_im + t_im * c_re
            c_re, c_im = e_re, e_im
        carry_ref[blk, :, 0:half] = jnp.broadcast_to(c_re, (SUBLANES, half))
        carry_ref[blk, :, half:2 * half] = jnp.broadcast_to(c_im, (SUBLANES, half))
        cin_re = jnp.concatenate(cin_re, axis=0)
        cin_im = jnp.concatenate(cin_im, axis=0)

        def fix_step(jj, _):
            out_re, out_im = [], []
            for k in range(2):
                j = 2 * jj + k
                r = pl.ds(pl.multiple_of(j * SUBLANES, SUBLANES), SUBLANES)
                p_re = apow_ref[blk, 0, pl.ds(j, 1), :]
                p_im = apow_ref[blk, 1, pl.ds(j, 1), :]
                out_re.append(s_ref[r, 0:half] + p_re * cin_re - p_im * cin_im)
                out_im.append(s_ref[r, half:2 * half] + p_re * cin_im + p_im * cin_re)
            r2 = pl.ds(pl.multiple_of(jj * 2 * SUBLANES, 2 * SUBLANES), 2 * SUBLANES)
            sb_ref[r2, 0:half] = jnp.concatenate(out_re, axis=0).astype(BF16)
            sb_ref[r2, half:2 * half] = jnp.concatenate(out_im, axis=0).astype(BF16)
            return 0

        lax.fori_loop(0, nj // 2, fix_step, 0, unroll=S5_UNROLL // 2)
        yp_ref[:, blk * bw:(blk + 1) * bw] = _dot(sb_ref[...], cmat_ref[blk])

    y = jax.nn.gelu(yp_ref[...] + dskip_ref[...] * u)
    z = y * jax.nn.sigmoid(_dot(y.astype(BF16), wg_ref[...]) + bg_ref[...])
    z = _dot(permt_ref[...], z.astype(BF16)).astype(BF16)
    o_ref[...] = h_ref[...] + _dot(z, wo_ref[...])


def _s5_discretise(lam_re, lam_im, log_dt, b_re, b_im, c_re, c_im, ts):
    groups, states = lam_re.shape
    gpb = S5_BLOCK_GROUPS
    nblk = groups // gpb
    lr = lam_re.astype(F32)
    li = lam_im.astype(F32)
    dt = jnp.exp(log_dt.astype(F32))[:, None]
    mag = jnp.exp(lr * dt)
    ab_re = mag * jnp.cos(li * dt)
    ab_im = mag * jnp.sin(li * dt)
    den = lr * lr + li * li
    z_re = ((ab_re - 1.0) * lr + ab_im * li) / den
    z_im = (ab_im * lr - (ab_re - 1.0) * li) / den
    br = b_re.astype(F32)
    bi = b_im.astype(F32)
    bb_re = z_re[..., None] * br - z_im[..., None] * bi
    bb_im = z_re[..., None] * bi + z_im[..., None] * br
    eye = jnp.eye(gpb, dtype=F32)

    def in_block(bb):
        t = bb.reshape(nblk, gpb, states, C_GROUP)
        return jnp.einsum('ngph,gk->nghkp', t, eye).reshape(nblk, gpb * C_GROUP, gpb * states)

    bmat = jnp.concatenate([in_block(bb_re), in_block(bb_im)], axis=-1).astype(BF16)

    def out_block(cc):
        t = cc.astype(F32).reshape(nblk, gpb, C_GROUP, states)
        return jnp.einsum('nghp,gk->ngpkh', t, eye).reshape(nblk, gpb * states, gpb * C_GROUP)

    cmat = jnp.concatenate([out_block(c_re), -out_block(c_im)], axis=1).astype(BF16)

    def powers(n):
        n = n.astype(F32)[:, None, None]
        m = jnp.exp(n * (lr * dt)[None])
        ang = n * (li * dt)[None]
        pr = (m * jnp.cos(ang)).reshape(-1, nblk, gpb * states).transpose(1, 0, 2)
        pi = (m * jnp.sin(ang)).reshape(-1, nblk, gpb * states).transpose(1, 0, 2)
        return jnp.stack([pr, pi], axis=1)

    nj = ts // SUBLANES
    acoef = jnp.broadcast_to(powers(jnp.ones((1,)))[:, :, 0:1, :], (nblk, 2, SUBLANES, gpb * states))
    apow = powers(jnp.arange(1, nj + 1))
    atile = powers(jnp.full((1,), nj))
    return bmat, cmat, acoef, apow, atile


def _s5_layer(h, gain, w_in, lam_re, lam_im, log_dt, b_re, b_im, c_re, c_im, d_skip, w_gate, b_gate,
              w_out, batch):
    rows, d = h.shape
    seq = rows // batch
    ts = S5_TILE
    nj = ts // SUBLANES
    per_batch = seq // ts
    bmat, cmat, acoef, apow, atile = _s5_discretise(lam_re, lam_im, log_dt, b_re, b_im, c_re, c_im, ts)
    nblk, _, sw = bmat.shape
    p = jnp.arange(ts)
    perm = (p[None, :] == ((p % SUBLANES) * nj + p // SUBLANES)[:, None]).astype(BF16)
    tile = lambda b, i: (b * per_batch + i, 0)
    return pl.pallas_call(
        _s5_kernel,
        grid=(batch, per_batch),
        in_specs=[pl.BlockSpec((ts, d), tile), _whole((1, d)), _whole((d, d)), _whole((ts, ts)), _whole((ts, ts)),
                  _whole(bmat.shape), _whole(acoef.shape), _whole(apow.shape), _whole(atile.shape),
                  _whole(cmat.shape), _whole((1, d)), _whole((d, d)), _whole((1, d)), _whole((d, d))],
        out_specs=pl.BlockSpec((ts, d), tile),
        out_shape=jax.ShapeDtypeStruct((rows, d), F32),
        scratch_shapes=[pltpu.VMEM((nblk, SUBLANES, sw), F32), pltpu.VMEM((ts, sw), F32),
                        pltpu.VMEM((ts, sw), F32), pltpu.VMEM((ts, sw), BF16),
                        pltpu.VMEM((ts, d), F32)],
        compiler_params=_params(("parallel", "arbitrary")),
        name="s5_mix",
    )(h, gain, w_in.astype(BF16), perm, perm.T, bmat, acoef, apow, atile, cmat, d_skip.reshape(1, d),
      w_gate.astype(BF16), b_gate.reshape(1, d), w_out.astype(BF16))


def kernel(x, mem, norm_mix, norm_x, norm_ffn, norm_mem, norm_final, a_w_in, a_conv_w, a_gate_b, a_norm, a_w_out, b_w_in, b_gate_w2, b_gate_b, b_norm, b_w_out, c_w_in, c_lam_re, c_lam_im, c_log_dt, c_b_re, c_b_im, c_c_re, c_c_im, c_d, c_w_gate, c_b_gate, c_w_out, x_w_q, x_w_kv, x_w_o, f_w_gate, f_w_up, f_w_down):
    batch, seq, d = x.shape
    mem_len = mem.shape[1]
    depth = norm_mix.shape[0]
    assert seq % ROW_TILE == 0 and seq % MLSTM_TILE == 0 and seq % GLA_TILE == 0 and seq % S5_TILE == 0
    assert MLSTM_TILE % MIX_CHUNK == 0 and GLA_TILE % GLA_CHUNK == 0 and seq % MLSTM_IN_TILE == 0
    assert f_w_gate.shape[2] % FF_CHUNK == 0 and d % (X_HEADS * LANES) == 0 and (batch * seq) % FF_TILE == 0

    kv = _mem_kv(mem.reshape(batch * mem_len, d), norm_mem.reshape(1, d), x_w_kv.astype(BF16), mem_len)

    h = x.reshape(batch * seq, d)
    for i in range(depth):
        kind = i % N_MIXERS
        j = i // N_MIXERS
        gain = norm_mix[i].reshape(1, d)
        if kind == 0:
            h = _mlstm_layer(h, gain, a_w_in[j], a_conv_w[j], a_gate_b[j], a_norm[j], a_w_out[j], batch)
        elif kind == 1:
            h = _gla_layer(h, gain, b_w_in[j], b_gate_w2[j], b_gate_b[j], b_norm[j], b_w_out[j], batch)
        else:
            h = _s5_layer(h, gain, c_w_in[j], c_lam_re[j], c_lam_im[j], c_log_dt[j], c_b_re[j], c_b_im[j],
                          c_c_re[j], c_c_im[j], c_d[j], c_w_gate[j], c_b_gate[j], c_w_out[j], batch)
        h = _xattn(h, norm_x[i].reshape(1, d), x_w_q[i].astype(BF16), kv, i, x_w_o[i].astype(BF16),
                   batch, mem_len)
        h = _swiglu(h, norm_ffn[i].reshape(1, d), f_w_gate[i].astype(BF16), f_w_up[i].astype(BF16),
                    f_w_down[i].astype(BF16), norm_final.reshape(1, d), final_norm=(i == depth - 1))
    return h.reshape(batch, seq, d)
```
